```python
import functools
import jax, jax.numpy as jnp
from jax import lax
import numpy as np

D_MODEL = 2048
BATCH = 2
SEQ = 4096
DEPTH = 4
DEC_BATCH = 8
DEC_SEQ = 8
PAST_LEN = 16384
PAGE_SIZE = 128

N_HEADS = 8
HEAD_DIM = 128
D_ATT = N_HEADS * HEAD_DIM
POOL_WINDOWS = (2, 4, 8, 16)
N_POOL_GROUPS = len(POOL_WINDOWS)
POOL_GROUP = D_MODEL // 8
D_POOL = N_POOL_GROUPS * POOL_GROUP
POOL_STATE = max(POOL_WINDOWS) - 1
D_IN = D_POOL + 3 * D_ATT
MOBA_BLOCK = 256
MOBA_TOPK = 3
Q_CHUNK = 16
D_FF = 4 * D_MODEL
ROPE_THETA = 10000.0
EPS = 1e-6

kernel_name = "hybrid_pool_moba_decoder_step"

F32 = jnp.float32


def rmsnorm(x, g):
    xf = x.astype(F32)
    y = xf * lax.rsqrt(jnp.mean(xf * xf, axis=-1, keepdims=True) + EPS)
    return (y * g.astype(F32)).astype(x.dtype)


def rope(x, pos):
    half = HEAD_DIM // 2
    inv = jnp.power(jnp.float32(ROPE_THETA), -jnp.arange(0, HEAD_DIM, 2, dtype=F32) / HEAD_DIM)
    ang = pos.astype(F32)[:, None] * inv[None, :]
    cos = jnp.cos(ang)[None, :, None, :]
    sin = jnp.sin(ang)[None, :, None, :]
    xf = x.astype(F32)
    x1, x2 = xf[..., :half], xf[..., half:]
    return jnp.concatenate([x1 * cos - x2 * sin, x2 * cos + x1 * sin], axis=-1).astype(x.dtype)


def pool_mix(u, prev, pos):
    B, L, _ = u.shape
    z = jnp.concatenate([prev.astype(F32), u.astype(F32)], axis=1)
    c = jnp.concatenate([jnp.zeros((B, 1, D_POOL), F32), jnp.cumsum(z, axis=1)], axis=1)
    end = c[:, POOL_STATE + 1:POOL_STATE + 1 + L]
    outs = []
    for g, w in enumerate(POOL_WINDOWS):
        sl = slice(g * POOL_GROUP, (g + 1) * POOL_GROUP)
        s = end[..., sl] - c[:, POOL_STATE + 1 - w:POOL_STATE + 1 - w + L, sl]
        cnt = jnp.minimum(w, pos + 1).astype(F32)[None, :, None]
        outs.append(s / cnt)
    mean = jnp.concatenate(outs, axis=-1)
    return (mean - u.astype(F32)).astype(u.dtype), z[:, -POOL_STATE:].astype(u.dtype)


def pool_branch(u, prev, pos, w_pool_grp, pool_scale):
    p, new_state = pool_mix(u, prev, pos)
    B, L, _ = p.shape
    pg = p.reshape(B, L, N_POOL_GROUPS, POOL_GROUP)
    y = jnp.einsum('blgc,gcd->blgd', pg, w_pool_grp).reshape(B, L, D_POOL) * pool_scale
    return y, new_state


def moba_attend(q, pos, kmean, fetch):
    B, Q, H, _ = q.shape
    nb = kmean.shape[1]
    topk = min(MOBA_TOPK, nb)
    own = pos // MOBA_BLOCK
    qf = q.astype(F32)
    gate = jnp.einsum('bqhd,bjhd->bqhj', qf, kmean.astype(F32))
    full_past = jnp.arange(nb)[None, :] < own[:, None]
    gate = jnp.where(full_past[None, :, None, :], gate, -jnp.inf)
    _, sel = lax.top_k(gate, topk)
    sel_ok = sel < own[None, :, None, None]
    own_b = jnp.broadcast_to(own[None, :, None, None], (B, Q, H, 1)).astype(sel.dtype)
    blocks = jnp.concatenate([sel, own_b], axis=-1)
    ok = jnp.concatenate([sel_ok, jnp.ones((B, Q, H, 1), bool)], axis=-1)
    k_g, v_g = fetch(blocks)
    key_pos = blocks[..., None] * MOBA_BLOCK + jnp.arange(MOBA_BLOCK)
    mask = ok[..., None] & (key_pos <= pos[None, :, None, None, None])
    logits = jnp.einsum('bqhd,bqhmkd->bqhmk', qf, k_g.astype(F32)) * (HEAD_DIM ** -0.5)
    logits = jnp.where(mask, logits, -jnp.inf)
    m = blocks.shape[-1]
    p = jax.nn.softmax(logits.reshape(B, Q, H, m * MOBA_BLOCK), axis=-1).reshape(B, Q, H, m, MOBA_BLOCK)
    out = jnp.einsum('bqhmk,bqhmkd->bqhd', p, v_g.astype(F32))
    return out.astype(q.dtype)


def prompt_attend(q, k, v, pos):
    B, S, H, Dh = q.shape
    nb = -(-S // MOBA_BLOCK)
    padn = nb * MOBA_BLOCK - S
    kb = jnp.pad(k, ((0, 0), (0, padn), (0, 0), (0, 0))).reshape(B, nb, MOBA_BLOCK, H, Dh)
    vb = jnp.pad(v, ((0, 0), (0, padn), (0, 0), (0, 0))).reshape(B, nb, MOBA_BLOCK, H, Dh)
    kmean = kb.astype(F32).sum(axis=2) / MOBA_BLOCK
    bi = jnp.arange(B)[:, None, None, None]
    hi = jnp.arange(H)[None, None, :, None]

    def fetch(blocks):
        return kb[bi, blocks, :, hi], vb[bi, blocks, :, hi]

    nc = S // Q_CHUNK
    qc = q.reshape(B, nc, Q_CHUNK, H, Dh).swapaxes(0, 1)
    pc = pos.reshape(nc, Q_CHUNK)
    out = lax.map(lambda a: moba_attend(a[0], a[1], kmean, fetch), (qc, pc))
    return out.swapaxes(0, 1).reshape(B, S, H, Dh)


def sample_attend(q, k, v, pos, cache_k, cache_v, page_table, layer):
    B, L, H, Dh = q.shape
    n_pages = page_table.shape[1]
    n_new = -(-L // PAGE_SIZE)
    padn = n_new * PAGE_SIZE - L
    kn = jnp.pad(k, ((0, 0), (0, padn), (0, 0), (0, 0))).reshape(B, n_new, PAGE_SIZE, H, Dh)
    vn = jnp.pad(v, ((0, 0), (0, padn), (0, 0), (0, 0))).reshape(B, n_new, PAGE_SIZE, H, Dh)
    ppb = MOBA_BLOCK // PAGE_SIZE
    page_sum = jnp.concatenate([cache_k[layer, page_table].astype(F32).sum(axis=2),
                                kn.astype(F32).sum(axis=2)], axis=1)
    nb = -(-(n_pages + n_new) // ppb)
    page_sum = jnp.pad(page_sum, ((0, 0), (0, nb * ppb - n_pages - n_new), (0, 0), (0, 0)))
    kmean = page_sum.reshape(B, nb, ppb, H, Dh).sum(axis=2) / MOBA_BLOCK
    bi = jnp.arange(B)[:, None, None, None, None]
    hi = jnp.arange(H)[None, None, :, None, None]
    li = jnp.full((1, 1, 1, 1, 1), layer, jnp.int32)

    def fetch(blocks):
        Bq, Q, Hq, m = blocks.shape
        lp = blocks[..., None] * ppb + jnp.arange(ppb)
        phys = page_table[bi, jnp.clip(lp, 0, n_pages - 1)]
        lpn = jnp.clip(lp - n_pages, 0, n_new - 1)
        in_cache = (lp < n_pages)[..., None, None]

        def rows(pool, new):
            r = jnp.where(in_cache, pool[li, phys, :, hi], new[bi, lpn, :, hi])
            return r.reshape(Bq, Q, Hq, m, MOBA_BLOCK, Dh)

        return rows(cache_k, kn), rows(cache_v, vn)

    return moba_attend(q, pos, kmean, fetch)


def trunk_layer(x, pool_prev, start, attend, g_mix, w_in, w_pool_grp, pool_scale, w_pool_proj,
                w_att_proj, w_gate, w_out, g_mlp, w_up, w_down):
    B, L, _ = x.shape
    h = rmsnorm(x, g_mix)
    proj = h @ w_in
    u = proj[..., :D_POOL]
    q, k, v = [proj[..., D_POOL + i * D_ATT:D_POOL + (i + 1) * D_ATT].reshape(B, L, N_HEADS, HEAD_DIM)
               for i in range(3)]
    pos = start + jnp.arange(L, dtype=jnp.int32)
    q = rope(q, pos)
    k = rope(k, pos)
    pool_out, pool_state = pool_branch(u, pool_prev, pos, w_pool_grp, pool_scale)
    att_out = attend(q, k, v, pos)
    gates = jax.nn.sigmoid((h @ w_gate).astype(F32))
    merged = (gates[..., :D_MODEL] * (pool_out @ w_pool_proj).astype(F32)
              + gates[..., D_MODEL:] * (att_out.reshape(B, L, D_ATT) @ w_att_proj).astype(F32))
    x = x + merged.astype(x.dtype) @ w_out
    h2 = rmsnorm(x, g_mlp)
    x = x + jnp.square(jax.nn.relu(h2 @ w_up)) @ w_down
    return x, k, v, pool_state


def setup_inputs(seed: int = 0) -> dict:
    key = jax.random.key(seed)
    ks = jax.random.split(key, 20)
    n_pages = PAST_LEN // PAGE_SIZE
    n_used = DEC_BATCH * n_pages
    n_pool_pages = n_used + n_used // 4
    nrm = jax.random.normal
    page_table = jax.random.permutation(ks[5], n_pool_pages)[:n_used].reshape(DEC_BATCH, n_pages).astype(jnp.int32)
    return {
        "x_prompt": nrm(ks[0], (BATCH, SEQ, D_MODEL), F32),
        "x_sample": nrm(ks[1], (DEC_BATCH, DEC_SEQ, D_MODEL), F32),
        "cache_k": nrm(ks[2], (DEPTH, n_pool_pages, PAGE_SIZE, N_HEADS, HEAD_DIM), F32),
        "cache_v": nrm(ks[3], (DEPTH, n_pool_pages, PAGE_SIZE, N_HEADS, HEAD_DIM), F32),
        "state_pool": nrm(ks[4], (DEPTH, DEC_BATCH, POOL_STATE, D_POOL), F32),
        "page_table": page_table,
        "g_mix": 1.0 + 0.1 * nrm(ks[6], (DEPTH, D_MODEL), F32),
        "w_in": nrm(ks[7], (DEPTH, D_MODEL, D_IN), F32) * D_MODEL ** -0.5,
        "w_pool_grp": nrm(ks[8], (DEPTH, N_POOL_GROUPS, POOL_GROUP, POOL_GROUP), F32) * POOL_GROUP ** -0.5,
        "pool_scale": 1.0 + 0.1 * nrm(ks[9], (DEPTH, D_POOL), F32),
        "w_pool_proj": nrm(ks[10], (DEPTH, D_POOL, D_MODEL), F32) * D_POOL ** -0.5,
        "w_att_proj": nrm(ks[11], (DEPTH, D_ATT, D_MODEL), F32) * D_ATT ** -0.5,
        "w_gate": nrm(ks[12], (DEPTH, D_MODEL, 2 * D_MODEL), F32) * D_MODEL ** -0.5,
        "w_out": nrm(ks[13], (DEPTH, D_MODEL, D_MODEL), F32) * D_MODEL ** -0.5,
        "g_mlp": 1.0 + 0.1 * nrm(ks[14], (DEPTH, D_MODEL), F32),
        "w_up": nrm(ks[15], (DEPTH, D_MODEL, D_FF), F32) * D_MODEL ** -0.5,
        "w_down": nrm(ks[16], (DEPTH, D_FF, D_MODEL), F32) * D_FF ** -0.5,
        "g_final": 1.0 + 0.1 * nrm(ks[17], (D_MODEL,), F32),
    }


def reference(x_prompt, x_sample, cache_k, cache_v, state_pool, page_table, g_mix, w_in, w_pool_grp,
              pool_scale, w_pool_proj, w_att_proj, w_gate, w_out, g_mlp, w_up, w_down, g_final):
    xp, xs = x_prompt, x_sample
    bp, sp_len = xp.shape[0], xp.shape[1]
    past = page_table.shape[1] * PAGE_SIZE
    zeros_prev = jnp.zeros((bp, POOL_STATE, D_POOL), xp.dtype)
    kp_l, vp_l, pp_l, ks_l, vs_l, ps_l = [], [], [], [], [], []
    for l in range(DEPTH):
        w = (g_mix[l], w_in[l], w_pool_grp[l], pool_scale[l], w_pool_proj[l], w_att_proj[l],
             w_gate[l], w_out[l], g_mlp[l], w_up[l], w_down[l])
        xp, kp, vp, pp = trunk_layer(xp, zeros_prev, 0, prompt_attend, *w)
        attend_s = functools.partial(sample_attend, cache_k=cache_k, cache_v=cache_v,
                                     page_table=page_table, layer=l)
        xs, kss, vss, pss = trunk_layer(xs, state_pool[l], past, attend_s, *w)
        kp_l.append(kp.reshape(bp, sp_len // PAGE_SIZE, PAGE_SIZE, N_HEADS, HEAD_DIM))
        vp_l.append(vp.reshape(bp, sp_len // PAGE_SIZE, PAGE_SIZE, N_HEADS, HEAD_DIM))
        pp_l.append(pp)
        ks_l.append(kss)
        vs_l.append(vss)
        ps_l.append(pss)
    y_prompt = rmsnorm(xp, g_final)
    y_sample = rmsnorm(xs, g_final)
    return (y_prompt, y_sample, jnp.stack(kp_l), jnp.stack(vp_l), jnp.stack(pp_l),
            jnp.stack(ks_l), jnp.stack(vs_l), jnp.stack(ps_l))
```

```python
import functools

import numpy as np
import jax
import jax.numpy as jnp
from jax import lax
from jax.experimental import pallas as pl
from jax.experimental.pallas import tpu as pltpu

F32 = jnp.float32
BF16 = jnp.bfloat16
I32 = jnp.int32
HIGHEST = lax.Precision.HIGHEST

N_HEADS = 8
HEAD_DIM = 128
POOL_WINDOWS = (2, 4, 8, 16)
POOL_GROUP = 256
POOL_STATE = 15
POOL_HALO = 16
MOBA_BLOCK = 256
MOBA_TOPK = 3
PAGE_SIZE = 128
PAGES_PER_BLOCK = MOBA_BLOCK // PAGE_SIZE
ROPE_THETA = 10000.0
EPS = 1e-6
NEG = -1e30
SM_SCALE = HEAD_DIM ** -0.5

VMEM_LIMIT = 52 * 1024 * 1024
NT_DIMS = (((1,), (1,)), ((), ()))


def _cparams(sem):
    return pltpu.CompilerParams(dimension_semantics=sem, vmem_limit_bytes=VMEM_LIMIT)


def _rms(x, g):
    ms = jnp.mean(x * x, axis=-1, keepdims=True)
    return x * lax.rsqrt(ms + EPS) * g


def _inproj_kernel(x_ref, g_ref, w_ref, cos_ref, sin_ref, h_ref, u_ref, q_ref, k_ref, v_ref):
    j = pl.program_id(1)

    @pl.when(j == 0)
    def _():
        h_ref[...] = _rms(x_ref[...], g_ref[...]).astype(BF16)

    acc = jnp.dot(h_ref[...], w_ref[...], preferred_element_type=F32)

    def rope(a):
        c = cos_ref[...]
        s = sin_ref[...]
        parts = []
        for hh in range(N_HEADS):
            ah = a[:, hh * HEAD_DIM:(hh + 1) * HEAD_DIM]
            parts.append(ah * c + pltpu.roll(ah, HEAD_DIM // 2, axis=1) * s)
        return jnp.concatenate(parts, axis=1)

    @pl.when(j == 0)
    def _():
        u_ref[...] = acc

    @pl.when(j == 1)
    def _():
        q_ref[...] = rope(acc)

    @pl.when(j == 2)
    def _():
        k_ref[...] = rope(acc)

    @pl.when(j == 3)
    def _():
        v_ref[...] = acc


def _inproj(x, g, w, cos, sin, tm):
    t, d = x.shape
    n = w.shape[1] // 4
    pos_blocks = cos.shape[0] // tm
    row = lambda i, j: (i, 0)
    tab = lambda i, j: (i % pos_blocks, 0)
    return pl.pallas_call(
        _inproj_kernel,
        grid=(t // tm, 4),
        in_specs=[
            pl.BlockSpec((tm, d), row),
            pl.BlockSpec((1, d), lambda i, j: (0, 0)),
            pl.BlockSpec((d, n), lambda i, j: (0, j)),
            pl.BlockSpec((tm, HEAD_DIM), tab),
            pl.BlockSpec((tm, HEAD_DIM), tab),
        ],
        out_specs=[
            pl.BlockSpec((tm, d), row),
            pl.BlockSpec((tm, n), row),
            pl.BlockSpec((tm, n), row),
            pl.BlockSpec((tm, n), row),
            pl.BlockSpec((tm, n), row),
        ],
        out_shape=[
            jax.ShapeDtypeStruct((t, d), BF16),
            jax.ShapeDtypeStruct((t, n), F32),
            jax.ShapeDtypeStruct((t, n), F32),
            jax.ShapeDtypeStruct((t, n), F32),
            jax.ShapeDtypeStruct((t, n), F32),
        ],
        compiler_params=_cparams(("arbitrary", "arbitrary")),
        name="inproj",
    )(x, g, w, cos, sin)


def _pool_kernel(*refs, ts, n_tiles, start):
    if n_tiles > 1:
        u_ref, halo_ref, prev_ref, wg_ref, sc_ref, y_ref, z = refs
    else:
        u_ref, prev_ref, wg_ref, sc_ref, y_ref, z = refs
    j = pl.program_id(1)
    z[POOL_HALO:POOL_HALO + ts, :] = u_ref[0]
    if n_tiles > 1:
        @pl.when(j == 0)
        def _():
            z[0:POOL_HALO, :] = prev_ref[0]

        @pl.when(j > 0)
        def _():
            z[0:POOL_HALO, :] = halo_ref[0]
    else:
        z[0:POOL_HALO, :] = prev_ref[0]

    pos = start + j * ts + lax.broadcasted_iota(I32, (ts, 1), 0)
    for g, w in enumerate(POOL_WINDOWS):
        cols = slice(g * POOL_GROUP, (g + 1) * POOL_GROUP)
        u = z[POOL_HALO:POOL_HALO + ts, cols]
        s = u
        for i in range(1, w):
            s = s + z[POOL_HALO - i:POOL_HALO - i + ts, cols]
        cnt = jnp.minimum(w, pos + 1).astype(F32)
        p = s / cnt - u
        y = jnp.dot(p.astype(BF16), wg_ref[g], preferred_element_type=F32) * sc_ref[:, cols]
        y_ref[0, :, cols] = y.astype(BF16)


def _pool(u, prev16, wg, scale, ts, start):
    b, s, dp = u.shape
    n_tiles = s // ts
    in_specs = [pl.BlockSpec((1, ts, dp), lambda bi, j: (bi, j, 0))]
    args = [u]
    if n_tiles > 1:
        hb = ts // POOL_HALO
        in_specs.append(pl.BlockSpec((1, POOL_HALO, dp), lambda bi, j: (bi, jnp.maximum(j * hb - 1, 0), 0)))
        args.append(u)
    in_specs += [
        pl.BlockSpec((1, POOL_HALO, dp), lambda bi, j: (bi, 0, 0)),
        pl.BlockSpec(wg.shape, lambda bi, j: (0, 0, 0)),
        pl.BlockSpec((1, dp), lambda bi, j: (0, 0)),
    ]
    args += [prev16, wg, scale]
    return pl.pallas_call(
        functools.partial(_pool_kernel, ts=ts, n_tiles=n_tiles, start=start),
        grid=(b, n_tiles),
        in_specs=in_specs,
        out_specs=pl.BlockSpec((1, ts, dp), lambda bi, j: (bi, j, 0)),
        out_shape=jax.ShapeDtypeStruct((b, s, dp), BF16),
        scratch_shapes=[pltpu.VMEM((POOL_HALO + ts, dp), F32)],
        compiler_params=_cparams(("arbitrary", "arbitrary")),
        name="pool",
    )(*args)


def _topk_mask(g_t, valid):
    nb = g_t.shape[0]
    jidx = lax.broadcasted_iota(I32, g_t.shape, 0)
    g = jnp.where(valid, g_t, -jnp.inf)
    rank = jnp.zeros(g_t.shape, I32)
    for j in range(nb):
        row = g[j:j + 1, :]
        beats = (row > g) | ((row == g) & (jidx > j))
        rank = rank + beats.astype(I32)
    return valid & (rank < MOBA_TOPK), rank


def _attn_prompt_kernel(q_ref, k_ref, v_ref, o_ref, kb, vt, kmean, selb):
    i = pl.program_id(2)
    nb = kb.shape[0]
    blk = MOBA_BLOCK

    @pl.when(i == 0)
    def _():
        for j in range(nb):
            kblk = k_ref[j * blk:(j + 1) * blk, :]
            kmean[j:j + 1, :] = jnp.sum(kblk, axis=0, keepdims=True) * (1.0 / blk)
            kb[j] = kblk.astype(BF16)
            vt[j] = v_ref[j * blk:(j + 1) * blk, :].T.astype(BF16)

    q_t = q_ref[...].T
    g_t = jnp.dot(kmean[...], q_t, precision=HIGHEST, preferred_element_type=F32)
    valid = lax.broadcasted_iota(I32, g_t.shape, 0) < i
    sel, _ = _topk_mask(g_t, valid)
    selb[...] = jnp.where(sel, 0.0, NEG)

    qs = (q_t * SM_SCALE).astype(BF16)
    r = lax.broadcasted_iota(I32, (blk, blk), 0)
    c = lax.broadcasted_iota(I32, (blk, blk), 1)
    s_t = jnp.dot(kb[i], qs, preferred_element_type=F32)
    s_t = jnp.where(r <= c, s_t, NEG)
    m = jnp.max(s_t, axis=0, keepdims=True)
    p = jnp.exp(s_t - m)
    l = jnp.sum(p, axis=0, keepdims=True)
    acc = jnp.dot(vt[i], p.astype(BF16), preferred_element_type=F32)

    def body(j, carry):
        m, l, acc = carry
        s_t = jnp.dot(kb[j], qs, preferred_element_type=F32) + selb[pl.ds(j, 1), :]
        m_new = jnp.maximum(m, jnp.max(s_t, axis=0, keepdims=True))
        alpha = jnp.exp(m - m_new)
        p = jnp.exp(s_t - m_new)
        l = alpha * l + jnp.sum(p, axis=0, keepdims=True)
        acc = alpha * acc + jnp.dot(vt[j], p.astype(BF16), preferred_element_type=F32)
        return m_new, l, acc

    m, l, acc = lax.fori_loop(0, i, body, (m, l, acc))
    o_ref[...] = (acc / l).T.astype(BF16)


def _attn_prompt(q, k, v, batch, seq):
    t, da = q.shape
    nb = seq // MOBA_BLOCK
    qmap = lambda b, h, i: (b * nb + i, h)
    kvmap = lambda b, h, i: (b, h)
    return pl.pallas_call(
        _attn_prompt_kernel,
        grid=(batch, N_HEADS, nb),
        in_specs=[
            pl.BlockSpec((MOBA_BLOCK, HEAD_DIM), qmap),
            pl.BlockSpec((seq, HEAD_DIM), kvmap),
            pl.BlockSpec((seq, HEAD_DIM), kvmap),
        ],
        out_specs=pl.BlockSpec((MOBA_BLOCK, HEAD_DIM), qmap),
        out_shape=jax.ShapeDtypeStruct((t, da), BF16),
        scratch_shapes=[
            pltpu.VMEM((nb, MOBA_BLOCK, HEAD_DIM), BF16),
            pltpu.VMEM((nb, HEAD_DIM, MOBA_BLOCK), BF16),
            pltpu.VMEM((nb, HEAD_DIM), F32),
            pltpu.VMEM((nb, MOBA_BLOCK), F32),
        ],
        compiler_params=_cparams(("arbitrary", "arbitrary", "arbitrary")),
        name="attn_prompt",
    )(q, k, v)


PAGE_RING = 4
BLOCKS_PER_GROUP = 8


def _pagesum_kernel(pt_ref, ck_hbm, out_ref, buf, sem, *, n_pages):
    l = pl.program_id(0)
    b = pl.program_id(1)
    pages_per_group = BLOCKS_PER_GROUP * PAGES_PER_BLOCK

    def copy(p, slot):
        return pltpu.make_async_copy(ck_hbm.at[l, pt_ref[b * n_pages + p]], buf.at[slot], sem.at[slot])

    for p in range(PAGE_RING - 1):
        copy(p, p).start()

    sub = lax.broadcasted_iota(I32, (BLOCKS_PER_GROUP, HEAD_DIM), 0)

    def group(gi, carry):
        sums = []
        for t in range(pages_per_group):
            p = gi * pages_per_group + t
            nxt = p + PAGE_RING - 1

            @pl.when(nxt < n_pages)
            def _():
                copy(nxt, (t + PAGE_RING - 1) % PAGE_RING).start()

            slot = t % PAGE_RING
            copy(p, slot).wait()
            acc = buf[slot, 0]
            for rr in range(1, PAGE_SIZE):
                acc = acc + buf[slot, rr]
            if t % PAGES_PER_BLOCK == 0:
                blk_sum = acc
            else:
                blk_sum = blk_sum + acc
                sums.append(blk_sum * (1.0 / MOBA_BLOCK))
        base = pl.multiple_of(gi * BLOCKS_PER_GROUP, BLOCKS_PER_GROUP)
        for hh in range(N_HEADS):
            tile = jnp.zeros((BLOCKS_PER_GROUP, HEAD_DIM), F32)
            for tb in range(BLOCKS_PER_GROUP):
                row = jnp.broadcast_to(sums[tb][hh:hh + 1, :], (BLOCKS_PER_GROUP, HEAD_DIM))
                tile = jnp.where(sub == tb, row, tile)
            out_ref[0, 0, hh, pl.ds(base, BLOCKS_PER_GROUP), :] = tile
        return carry

    lax.fori_loop(0, n_pages // pages_per_group, group, 0)


def _pagesum(pt_flat, cache_k, n_seq, n_pages):
    depth = cache_k.shape[0]
    nblk = n_pages // PAGES_PER_BLOCK
    return pl.pallas_call(
        functools.partial(_pagesum_kernel, n_pages=n_pages),
        grid_spec=pltpu.PrefetchScalarGridSpec(
            num_scalar_prefetch=1,
            grid=(depth, n_seq),
            in_specs=[pl.BlockSpec(memory_space=pl.ANY)],
            out_specs=pl.BlockSpec((1, 1, N_HEADS, nblk, HEAD_DIM), lambda l, b, pt: (l, b, 0, 0, 0)),
            scratch_shapes=[
                pltpu.VMEM((PAGE_RING, PAGE_SIZE, N_HEADS, HEAD_DIM), F32),
                pltpu.SemaphoreType.DMA((PAGE_RING,)),
            ],
        ),
        out_shape=jax.ShapeDtypeStruct((depth, n_seq, N_HEADS, nblk, HEAD_DIM), F32),
        compiler_params=_cparams(("arbitrary", "arbitrary")),
        name="pagesum",
    )(pt_flat, cache_k)


def _gate_kernel(km_ref, q_ref, sel_ref):
    lq = q_ref.shape[0]
    nblk = km_ref.shape[3]
    ncol = N_HEADS * lq
    rows = lax.broadcasted_iota(I32, (ncol, HEAD_DIM), 0)
    g_t = jnp.zeros((nblk, ncol), F32)
    for hh in range(N_HEADS):
        qh = q_ref[:, hh * HEAD_DIM:(hh + 1) * HEAD_DIM]
        qpad = jnp.concatenate([qh] * N_HEADS, axis=0)
        qpad = jnp.where((rows >= hh * lq) & (rows < (hh + 1) * lq), qpad, 0.0)
        g_t = g_t + lax.dot_general(km_ref[0, 0, hh], qpad, NT_DIMS, precision=HIGHEST,
                                    preferred_element_type=F32)
    valid = jnp.full(g_t.shape, True)
    _, rank = _topk_mask(g_t, valid)
    bidx = lax.broadcasted_iota(I32, g_t.shape, 0).astype(F32)
    out = []
    for r in range(MOBA_TOPK):
        out.append(jnp.sum(jnp.where(rank == r, bidx, 0.0), axis=0, keepdims=True))
    out.append(jnp.zeros((8 - MOBA_TOPK, ncol), F32))
    sel_ref[0] = jnp.concatenate(out, axis=0).astype(I32)


def _gate(kmean_all, q, layer, n_seq, lq):
    nblk = kmean_all.shape[3]
    ncol = N_HEADS * lq
    return pl.pallas_call(
        _gate_kernel,
        grid=(n_seq,),
        in_specs=[
            pl.BlockSpec((1, 1, N_HEADS, nblk, HEAD_DIM), lambda b: (layer, b, 0, 0, 0)),
            pl.BlockSpec((lq, N_HEADS * HEAD_DIM), lambda b: (b, 0)),
        ],
        out_specs=pl.BlockSpec((1, 8, ncol), lambda b: (b, 0, 0)),
        out_shape=jax.ShapeDtypeStruct((n_seq, 8, ncol), I32),
        compiler_params=_cparams(("arbitrary",)),
        name="gate",
    )(kmean_all, q)


def _sattn_kernel(pt_ref, sel_ref, q_ref, kn_ref, vn_ref, ck_hbm, cv_hbm, o_ref, kbuf, vbuf, sem,
                  *, layer, n_pages, lq):
    s = pl.program_id(0)
    nsteps = pl.num_programs(0)
    nseg = lq * MOBA_TOPK
    seg_rows = MOBA_TOPK * MOBA_BLOCK

    def copies(step, slot):
        b = lax.shift_right_logical(step, 3)
        hh = step & (N_HEADS - 1)
        out = []
        for seg in range(nseg):
            blk = sel_ref[step * nseg + seg]
            for half in range(PAGES_PER_BLOCK):
                page = pt_ref[b * n_pages + blk * PAGES_PER_BLOCK + half]
                dst = pl.ds(seg * MOBA_BLOCK + half * PAGE_SIZE, PAGE_SIZE)
                out.append(pltpu.make_async_copy(ck_hbm.at[layer, page, :, hh, :], kbuf.at[slot, dst, :], sem.at[0, slot]))
                out.append(pltpu.make_async_copy(cv_hbm.at[layer, page, :, hh, :], vbuf.at[slot, dst, :], sem.at[1, slot]))
        return out

    @pl.when(s == 0)
    def _():
        for cp in copies(s, 0):
            cp.start()

    @pl.when(s + 1 < nsteps)
    def _():
        for cp in copies(s + 1, (s + 1) & 1):
            cp.start()

    slot = s & 1
    for cp in copies(s, slot):
        cp.wait()

    qb = (q_ref[...] * SM_SCALE).astype(BF16)
    s_t = lax.dot_general(kbuf[slot].astype(BF16), qb, NT_DIMS, preferred_element_type=F32)
    key = lax.broadcasted_iota(I32, s_t.shape, 0)
    col = lax.broadcasted_iota(I32, s_t.shape, 1)
    mine = (key >= col * seg_rows) & (key < (col + 1) * seg_rows)
    s_t = jnp.where(mine, s_t, NEG)
    sn_t = lax.dot_general(kn_ref[...].astype(BF16), qb, NT_DIMS, preferred_element_type=F32)
    rn = lax.broadcasted_iota(I32, sn_t.shape, 0)
    cn = lax.broadcasted_iota(I32, sn_t.shape, 1)
    sn_t = jnp.where(rn <= cn, sn_t, NEG)
    m = jnp.maximum(jnp.max(s_t, axis=0, keepdims=True), jnp.max(sn_t, axis=0, keepdims=True))
    p = jnp.exp(s_t - m)
    pn = jnp.exp(sn_t - m)
    inv_l = 1.0 / (jnp.sum(p, axis=0, keepdims=True) + jnp.sum(pn, axis=0, keepdims=True))
    p = p * inv_l
    pn = pn * inv_l
    vn = vn_ref[...]
    rows = []
    for qq in range(lq):
        seg = slice(qq * seg_rows, (qq + 1) * seg_rows)
        o = jnp.sum(p[seg, qq:qq + 1] * vbuf[slot, seg, :], axis=0, keepdims=True)
        o = o + jnp.sum(pn[:, qq:qq + 1] * vn, axis=0, keepdims=True)
        rows.append(o)
    o_ref[...] = jnp.concatenate(rows, axis=0)


def _sattn(pt_flat, sel_flat, q, kn, vn, cache_k, cache_v, layer, n_seq, lq, n_pages):
    nseg = lq * MOBA_TOPK
    blk = lambda s, pt, sel: (lax.shift_right_logical(s, 3), s & (N_HEADS - 1))
    spec = pl.BlockSpec((lq, HEAD_DIM), blk)
    return pl.pallas_call(
        functools.partial(_sattn_kernel, layer=layer, n_pages=n_pages, lq=lq),
        grid_spec=pltpu.PrefetchScalarGridSpec(
            num_scalar_prefetch=2,
            grid=(n_seq * N_HEADS,),
            in_specs=[spec, spec, spec, pl.BlockSpec(memory_space=pl.ANY), pl.BlockSpec(memory_space=pl.ANY)],
            out_specs=spec,
            scratch_shapes=[
                pltpu.VMEM((2, nseg * MOBA_BLOCK, HEAD_DIM), F32),
                pltpu.VMEM((2, nseg * MOBA_BLOCK, HEAD_DIM), F32),
                pltpu.SemaphoreType.DMA((2, 2)),
            ],
        ),
        out_shape=jax.ShapeDtypeStruct(q.shape, F32),
        compiler_params=_cparams(("arbitrary",)),
        name="sattn",
    )(pt_flat, sel_flat, q, kn, vn, cache_k, cache_v)


def _merge_kernel(h_ref, wg1_ref, wg2_ref, py_ref, wpp_ref, at_ref, wap_ref, o_ref):
    h = h_ref[...]
    g1 = jax.nn.sigmoid(jnp.dot(h, wg1_ref[...], preferred_element_type=F32))
    g2 = jax.nn.sigmoid(jnp.dot(h, wg2_ref[...], preferred_element_type=F32))
    a = jnp.dot(py_ref[...], wpp_ref[...], preferred_element_type=F32)
    b = jnp.dot(at_ref[...], wap_ref[...], preferred_element_type=F32)
    o_ref[...] = (g1 * a + g2 * b).astype(BF16)


def _merge(h, wgate, py, wpp, at, wap, tm, tn):
    t, d = h.shape
    dp = py.shape[1]
    da = at.shape[1]
    nj = d // tn
    row = lambda i, j: (i, 0)
    return pl.pallas_call(
        _merge_kernel,
        grid=(t // tm, nj),
        in_specs=[
            pl.BlockSpec((tm, d), row),
            pl.BlockSpec((d, tn), lambda i, j: (0, j)),
            pl.BlockSpec((d, tn), lambda i, j: (0, j + nj)),
            pl.BlockSpec((tm, dp), row),
            pl.BlockSpec((dp, tn), lambda i, j: (0, j)),
            pl.BlockSpec((tm, da), row),
            pl.BlockSpec((da, tn), lambda i, j: (0, j)),
        ],
        out_specs=pl.BlockSpec((tm, tn), lambda i, j: (i, j)),
        out_shape=jax.ShapeDtypeStruct((t, d), BF16),
        compiler_params=_cparams(("arbitrary", "arbitrary")),
        name="merge",
    )(h, wgate, wgate, py, wpp, at, wap)


def _resproj_kernel(a_ref, w_ref, x_ref, o_ref):
    k = pl.program_id(2)
    acc = jnp.dot(a_ref[...], w_ref[...], preferred_element_type=F32)

    @pl.when(k == 0)
    def _():
        o_ref[...] = x_ref[...] + acc

    @pl.when(k > 0)
    def _():
        o_ref[...] = o_ref[...] + acc


def _resproj(a, w, x, tm, tn, tk):
    t, kd = a.shape
    n = w.shape[1]
    return pl.pallas_call(
        _resproj_kernel,
        grid=(t // tm, n // tn, kd // tk),
        in_specs=[
            pl.BlockSpec((tm, tk), lambda i, j, k: (i, k)),
            pl.BlockSpec((tk, tn), lambda i, j, k: (k, j)),
            pl.BlockSpec((tm, tn), lambda i, j, k: (i, j)),
        ],
        out_specs=pl.BlockSpec((tm, tn), lambda i, j, k: (i, j)),
        out_shape=jax.ShapeDtypeStruct((t, n), F32),
        compiler_params=_cparams(("arbitrary", "arbitrary", "arbitrary")),
        name="resproj",
    )(a, w, x)


def _mlp_up_kernel(x_ref, g_ref, w_ref, o_ref, h2):
    @pl.when(pl.program_id(1) == 0)
    def _():
        h2[...] = _rms(x_ref[...], g_ref[...]).astype(BF16)

    acc = jnp.dot(h2[...], w_ref[...], preferred_element_type=F32)
    o_ref[...] = jnp.square(jnp.maximum(acc, 0.0)).astype(BF16)


def _mlp_up(x, g, w, tm, tn):
    t, d = x.shape
    n = w.shape[1]
    return pl.pallas_call(
        _mlp_up_kernel,
        grid=(t // tm, n // tn),
        in_specs=[
            pl.BlockSpec((tm, d), lambda i, j: (i, 0)),
            pl.BlockSpec((1, d), lambda i, j: (0, 0)),
            pl.BlockSpec((d, tn), lambda i, j: (0, j)),
        ],
        out_specs=pl.BlockSpec((tm, tn), lambda i, j: (i, j)),
        out_shape=jax.ShapeDtypeStruct((t, n), BF16),
        scratch_shapes=[pltpu.VMEM((tm, d), BF16)],
        compiler_params=_cparams(("arbitrary", "arbitrary")),
        name="mlp_up",
    )(x, g, w)


def _final_norm_kernel(x_ref, g_ref, o_ref):
    o_ref[...] = _rms(x_ref[...], g_ref[...])


def _final_norm(x, g, tm):
    t, d = x.shape
    return pl.pallas_call(
        _final_norm_kernel,
        grid=(t // tm,),
        in_specs=[pl.BlockSpec((tm, d), lambda i: (i, 0)), pl.BlockSpec((1, d), lambda i: (0, 0))],
        out_specs=pl.BlockSpec((tm, d), lambda i: (i, 0)),
        out_shape=jax.ShapeDtypeStruct((t, d), F32),
        compiler_params=_cparams(("arbitrary",)),
        name="final_norm",
    )(x, g)


def _rope_tables(pos):
    inv = ROPE_THETA ** (-np.arange(0, HEAD_DIM, 2, dtype=np.float64) / HEAD_DIM)
    ang = np.asarray(pos, np.float64)[:, None] * inv[None, :]
    cos = np.concatenate([np.cos(ang), np.cos(ang)], axis=1)
    sin = np.concatenate([-np.sin(ang), np.sin(ang)], axis=1)
    return jnp.asarray(cos, F32), jnp.asarray(sin, F32)


def _row_layers(x, h, py, at, weights, tm):
    wgate, wpp, wap, wout, gmlp, wup, wdown = weights
    d = x.shape[1]
    tn = min(512, d)
    merged = _merge(h, wgate, py, wpp, at, wap, tm, tn)
    x1 = _resproj(merged, wout, x, tm, min(1024, d), d)
    a = _mlp_up(x1, gmlp, wup, tm, 1024)
    return _resproj(a, wdown, x1, tm, min(1024, d), 2048)


def kernel(x_prompt, x_sample, cache_k, cache_v, state_pool, page_table, g_mix, w_in, w_pool_grp, pool_scale, w_pool_proj, w_att_proj, w_gate, w_out, g_mlp, w_up, w_down, g_final):
    bp, sp, d = x_prompt.shape
    bs, ls, _ = x_sample.shape
    depth = w_in.shape[0]
    n_pages = page_table.shape[1]
    past = n_pages * PAGE_SIZE
    dp = pool_scale.shape[1]
    tp = 512
    ts_rows = bs * ls

    xp = x_prompt.reshape(bp * sp, d)
    xs = x_sample.reshape(ts_rows, d)
    cos_p, sin_p = _rope_tables(np.arange(sp))
    cos_s, sin_s = _rope_tables(np.tile(past + np.arange(ls), bs))
    pt_flat = page_table.reshape(-1)
    kmean_all = _pagesum(pt_flat, cache_k, bs, n_pages)
    prev_p = jnp.zeros((bp, POOL_HALO, dp), F32)

    kp_l, vp_l, pp_l, ks_l, vs_l, ps_l = [], [], [], [], [], []
    for l in range(depth):
        gm = g_mix[l].reshape(1, d)
        win = w_in[l].astype(BF16)
        wgrp = w_pool_grp[l].astype(BF16)
        psc = pool_scale[l].reshape(1, dp)
        rest = (w_gate[l].astype(BF16), w_pool_proj[l].astype(BF16), w_att_proj[l].astype(BF16),
                w_out[l].astype(BF16), g_mlp[l].reshape(1, d), w_up[l].astype(BF16), w_down[l].astype(BF16))

        hp, up, qp, kp, vp = _inproj(xp, gm, win, cos_p, sin_p, tp)
        up3 = up.reshape(bp, sp, dp)
        pyp = _pool(up3, prev_p, wgrp, psc, tp, 0).reshape(bp * sp, dp)
        atp = _attn_prompt(qp, kp, vp, bp, sp)
        xp = _row_layers(xp, hp, pyp, atp, rest, tp)
        kp_l.append(kp.reshape(bp, sp // PAGE_SIZE, PAGE_SIZE, N_HEADS, HEAD_DIM))
        vp_l.append(vp.reshape(bp, sp // PAGE_SIZE, PAGE_SIZE, N_HEADS, HEAD_DIM))
        pp_l.append(up3[:, sp - POOL_STATE:, :])

        hs, us, qs, ks, vs = _inproj(xs, gm, win, cos_s, sin_s, ts_rows)
        us3 = us.reshape(bs, ls, dp)
        prev_s = jnp.pad(state_pool[l], ((0, 0), (POOL_HALO - POOL_STATE, 0), (0, 0)))
        pys = _pool(us3, prev_s, wgrp, psc, ls, past).reshape(ts_rows, dp)
        sel = _gate(kmean_all, qs, l, bs, ls)
        sel_flat = sel[:, :MOBA_TOPK, :].transpose(0, 2, 1).reshape(-1)
        ats = _sattn(pt_flat, sel_flat, qs, ks, vs, cache_k, cache_v, l, bs, ls, n_pages).astype(BF16)
        xs = _row_layers(xs, hs, pys, ats, rest, ts_rows)
        ks_l.append(ks.reshape(bs, ls, N_HEADS, HEAD_DIM))
        vs_l.append(vs.reshape(bs, ls, N_HEADS, HEAD_DIM))
        ps_l.append(jnp.concatenate([state_pool[l], us3], axis=1)[:, -POOL_STATE:, :])

    gf = g_final.reshape(1, d)
    y_prompt = _final_norm(xp, gf, tp).reshape(bp, sp, d)
    y_sample = _final_norm(xs, gf, ts_rows).reshape(bs, ls, d)
    return (y_prompt, y_sample, jnp.stack(kp_l), jnp.stack(vp_l), jnp.stack(pp_l),
            jnp.stack(ks_l), jnp.stack(vs_l), jnp.stack(ps_l))
```

```python
import functools

import numpy as np
import jax
import jax.numpy as jnp
from jax import lax
from jax.experimental import pallas as pl
from jax.experimental.pallas import tpu as pltpu

F32 = jnp.float32
BF16 = jnp.bfloat16
I32 = jnp.int32
HIGHEST = lax.Precision.HIGHEST

N_HEADS = 8
HEAD_DIM = 128
POOL_WINDOWS = (2, 4, 8, 16)
POOL_GROUP = 256
POOL_STATE = 15
POOL_HALO = 16
MOBA_BLOCK = 256
MOBA_TOPK = 3
PAGE_SIZE = 128
PAGES_PER_BLOCK = MOBA_BLOCK // PAGE_SIZE
ROPE_THETA = 10000.0
EPS = 1e-6
NEG = -1e30
SM_SCALE = HEAD_DIM ** -0.5
LOG2E = 1.4426950408889634

VMEM_LIMIT = 52 * 1024 * 1024
NT_DIMS = (((1,), (1,)), ((), ()))


def _cparams(sem):
    return pltpu.CompilerParams(dimension_semantics=sem, vmem_limit_bytes=VMEM_LIMIT)


def _rms(x, g):
    ms = jnp.mean(x * x, axis=-1, keepdims=True)
    return x * lax.rsqrt(ms + EPS) * g


def _inproj_kernel(x_ref, g_ref, w_ref, cos_ref, sin_ref, h_ref, u_ref, q_ref, k_ref, v_ref, kb_ref, vb_ref):
    j = pl.program_id(1)

    @pl.when(j == 0)
    def _():
        h_ref[...] = _rms(x_ref[...], g_ref[...]).astype(BF16)

    acc = jnp.dot(h_ref[...], w_ref[...], preferred_element_type=F32)

    def rope(a):
        c = cos_ref[...]
        s = sin_ref[...]
        parts = []
        for hh in range(N_HEADS):
            ah = a[:, hh * HEAD_DIM:(hh + 1) * HEAD_DIM]
            parts.append(ah * c + pltpu.roll(ah, HEAD_DIM // 2, axis=1) * s)
        return jnp.concatenate(parts, axis=1)

    @pl.when(j == 0)
    def _():
        u_ref[...] = acc

    @pl.when(j == 1)
    def _():
        q_ref[...] = rope(acc)

    @pl.when(j == 2)
    def _():
        k = rope(acc)
        k_ref[...] = k
        kb_ref[...] = k.astype(BF16)

    @pl.when(j == 3)
    def _():
        v_ref[...] = acc
        vb_ref[...] = acc.astype(BF16)


def _inproj(x, g, w, cos, sin, layer, tm):
    t, d = x.shape
    n = w.shape[2] // 4
    pos_blocks = cos.shape[0] // tm
    row = lambda i, j: (i, 0)
    tab = lambda i, j: (i % pos_blocks, 0)
    return pl.pallas_call(
        _inproj_kernel,
        grid=(t // tm, 4),
        in_specs=[
            pl.BlockSpec((tm, d), row),
            pl.BlockSpec((None, 1, d), lambda i, j: (layer, 0, 0)),
            pl.BlockSpec((None, d, n), lambda i, j: (layer, 0, j)),
            pl.BlockSpec((tm, HEAD_DIM), tab),
            pl.BlockSpec((tm, HEAD_DIM), tab),
        ],
        out_specs=[pl.BlockSpec((tm, d), row)] + [pl.BlockSpec((tm, n), row)] * 6,
        out_shape=[jax.ShapeDtypeStruct((t, d), BF16)]
        + [jax.ShapeDtypeStruct((t, n), F32)] * 4
        + [jax.ShapeDtypeStruct((t, n), BF16)] * 2,
        compiler_params=_cparams(("arbitrary", "arbitrary")),
        name="inproj",
    )(x, g, w, cos, sin)


def _pool_kernel(*refs, ts, n_tiles, start):
    if n_tiles > 1:
        u_ref, halo_ref, prev_ref, wg_ref, sc_ref, y_ref, z = refs
    else:
        u_ref, prev_ref, wg_ref, sc_ref, y_ref, z = refs
    j = pl.program_id(1)
    z[POOL_HALO:POOL_HALO + ts, :] = u_ref[0]
    if n_tiles > 1:
        @pl.when(j == 0)
        def _():
            z[0:POOL_HALO, :] = prev_ref[0]

        @pl.when(j > 0)
        def _():
            z[0:POOL_HALO, :] = halo_ref[0]
    else:
        z[0:POOL_HALO, :] = prev_ref[0]

    pos = start + j * ts + lax.broadcasted_iota(I32, (ts, 1), 0)
    for g, w in enumerate(POOL_WINDOWS):
        cols = slice(g * POOL_GROUP, (g + 1) * POOL_GROUP)
        u = z[POOL_HALO:POOL_HALO + ts, cols]
        s = u
        for i in range(1, w):
            s = s + z[POOL_HALO - i:POOL_HALO - i + ts, cols]
        cnt = jnp.minimum(w, pos + 1).astype(F32)
        p = s / cnt - u
        y = jnp.dot(p.astype(BF16), wg_ref[g], preferred_element_type=F32) * sc_ref[:, cols]
        y_ref[0, :, cols] = y.astype(BF16)


def _pool(u, prev16, wg, scale, layer, ts, start):
    b, s, dp = u.shape
    n_tiles = s // ts
    in_specs = [pl.BlockSpec((1, ts, dp), lambda bi, j: (bi, j, 0))]
    args = [u]
    if n_tiles > 1:
        hb = ts // POOL_HALO
        in_specs.append(pl.BlockSpec((1, POOL_HALO, dp), lambda bi, j: (bi, jnp.maximum(j * hb - 1, 0), 0)))
        args.append(u)
    in_specs += [
        pl.BlockSpec((1, POOL_HALO, dp), lambda bi, j: (bi, 0, 0)),
        pl.BlockSpec((None,) + wg.shape[1:], lambda bi, j: (layer, 0, 0, 0)),
        pl.BlockSpec((None, 1, dp), lambda bi, j: (layer, 0, 0)),
    ]
    args += [prev16, wg, scale]
    return pl.pallas_call(
        functools.partial(_pool_kernel, ts=ts, n_tiles=n_tiles, start=start),
        grid=(b, n_tiles),
        in_specs=in_specs,
        out_specs=pl.BlockSpec((1, ts, dp), lambda bi, j: (bi, j, 0)),
        out_shape=jax.ShapeDtypeStruct((b, s, dp), BF16),
        scratch_shapes=[pltpu.VMEM((POOL_HALO + ts, dp), F32)],
        compiler_params=_cparams(("arbitrary", "arbitrary")),
        name="pool",
    )(*args)


def _topk_mask(g_t, valid):
    nb = g_t.shape[0]
    jidx = lax.broadcasted_iota(I32, g_t.shape, 0)
    g = jnp.where(valid, g_t, -jnp.inf)
    rank = jnp.zeros(g_t.shape, I32)
    for j in range(nb):
        row = g[j:j + 1, :]
        beats = (row > g) | ((row == g) & (jidx > j))
        rank = rank + beats.astype(I32)
    return valid & (rank < MOBA_TOPK), rank


ATTN_HEADS_PER_STEP = 2
ATTN_CLASS_BLOCKS = 4
KMEAN_BLOCKS_PER_STEP = 8


def _attn_prompt_kernel(q_ref, kb_ref, vb_ref, km_ref, o_ref, vt, s_scr, p_scr, *, hg, nb):
    i = pl.program_id(2)
    blk = MOBA_BLOCK
    hcols = lambda hh: slice(hh * HEAD_DIM, (hh + 1) * HEAD_DIM)

    @pl.when(i == 0)
    def _():
        for hh in range(hg):
            for j in range(nb):
                vblk = vb_ref[j * blk:(j + 1) * blk, hcols(hh)]
                vt[hh, :, j * blk:(j + 1) * blk] = vblk.astype(F32).T.astype(BF16)

    rc = lax.broadcasted_iota(I32, (blk, blk), 0) - lax.broadcasted_iota(I32, (blk, blk), 1)
    jidx = lax.broadcasted_iota(I32, (nb, blk), 0)

    def attend(hh, n, bias, qs):
        m = None
        for j in range(n):
            s = jnp.dot(kb_ref[j * blk:(j + 1) * blk, hcols(hh)], qs, preferred_element_type=F32)
            s = s + bias[j:j + 1, :]
            if j >= n - ATTN_CLASS_BLOCKS:
                s = jnp.where(rc <= jnp.where(i == j, 0, blk), s, NEG)
            s_scr[hh, j * blk:(j + 1) * blk, :] = s
            cm = jnp.max(s, axis=0, keepdims=True)
            m = cm if m is None else jnp.maximum(m, cm)
        l = jnp.zeros_like(m)
        for j in range(n):
            p = jnp.exp2(s_scr[hh, j * blk:(j + 1) * blk, :] - m)
            l = l + jnp.sum(p, axis=0, keepdims=True)
            p_scr[hh, j * blk:(j + 1) * blk, :] = p.astype(BF16)
        acc = jnp.dot(vt[hh, :, 0:n * blk], p_scr[hh, 0:n * blk, :], preferred_element_type=F32)
        o_ref[:, hcols(hh)] = (acc / l).T.astype(BF16)

    for hh in range(hg):
        q_t = q_ref[:, hcols(hh)].T
        g_t = jnp.dot(km_ref[:, hcols(hh)], q_t, precision=HIGHEST, preferred_element_type=F32)
        sel, _ = _topk_mask(g_t, jidx < i)
        bias = jnp.where(sel | (jidx == i), 0.0, NEG)
        qs = (q_t * (SM_SCALE * LOG2E)).astype(BF16)
        for n in range(ATTN_CLASS_BLOCKS, nb + 1, ATTN_CLASS_BLOCKS):
            @pl.when((i >= n - ATTN_CLASS_BLOCKS) & (i < n))
            def _(hh=hh, n=n, bias=bias, qs=qs):
                attend(hh, n, bias, qs)


def _attn_prompt(q, kb, vb, kmean, batch, seq):
    t, da = q.shape
    nb = seq // MOBA_BLOCK
    hg = ATTN_HEADS_PER_STEP
    wcols = hg * HEAD_DIM
    qmap = lambda b, g, i: (b * nb + i, g)
    kvmap = lambda b, g, i: (b, g)
    return pl.pallas_call(
        functools.partial(_attn_prompt_kernel, hg=hg, nb=nb),
        grid=(batch, N_HEADS // hg, nb),
        in_specs=[
            pl.BlockSpec((MOBA_BLOCK, wcols), qmap),
            pl.BlockSpec((seq, wcols), kvmap),
            pl.BlockSpec((seq, wcols), kvmap),
            pl.BlockSpec((nb, wcols), kvmap),
        ],
        out_specs=pl.BlockSpec((MOBA_BLOCK, wcols), qmap),
        out_shape=jax.ShapeDtypeStruct((t, da), BF16),
        scratch_shapes=[
            pltpu.VMEM((hg, HEAD_DIM, seq), BF16),
            pltpu.VMEM((hg, seq, MOBA_BLOCK), F32),
            pltpu.VMEM((hg, seq, MOBA_BLOCK), BF16),
        ],
        compiler_params=_cparams(("arbitrary", "arbitrary", "arbitrary")),
        name="attn_prompt",
    )(q, kb, vb, kmean)


def _kmean_kernel(k_ref, o_ref):
    rows = []
    for j in range(KMEAN_BLOCKS_PER_STEP):
        rows.append(jnp.sum(k_ref[j * MOBA_BLOCK:(j + 1) * MOBA_BLOCK, :], axis=0, keepdims=True))
    o_ref[...] = jnp.concatenate(rows, axis=0) * (1.0 / MOBA_BLOCK)


def _kmean(k):
    t, da = k.shape
    rows = KMEAN_BLOCKS_PER_STEP * MOBA_BLOCK
    return pl.pallas_call(
        _kmean_kernel,
        grid=(t // rows,),
        in_specs=[pl.BlockSpec((rows, da), lambda i: (i, 0))],
        out_specs=pl.BlockSpec((KMEAN_BLOCKS_PER_STEP, da), lambda i: (i, 0)),
        out_shape=jax.ShapeDtypeStruct((t // MOBA_BLOCK, da), F32),
        compiler_params=_cparams(("arbitrary",)),
        name="kmean",
    )(k)


PAGE_RING = 16
BLOCKS_PER_GROUP = 8


def _pagesum_kernel(pt_ref, ck_hbm, out_ref, buf, sem, *, n_pages):
    l = pl.program_id(0)
    b = pl.program_id(1)
    pages_per_group = BLOCKS_PER_GROUP * PAGES_PER_BLOCK
    assert pages_per_group % PAGE_RING == 0

    def copy(p, slot):
        return pltpu.make_async_copy(ck_hbm.at[l, pt_ref[b * n_pages + p]], buf.at[slot], sem.at[slot])

    for p in range(PAGE_RING - 1):
        copy(p, p).start()

    sub = lax.broadcasted_iota(I32, (BLOCKS_PER_GROUP, HEAD_DIM), 0)

    def group(gi, carry):
        sums = []
        for t in range(pages_per_group):
            p = gi * pages_per_group + t
            nxt = p + PAGE_RING - 1

            @pl.when(nxt < n_pages)
            def _():
                copy(nxt, (t + PAGE_RING - 1) % PAGE_RING).start()

            slot = t % PAGE_RING
            copy(p, slot).wait()
            parts = [buf[slot, rr] for rr in range(4)]
            for rr in range(4, PAGE_SIZE):
                parts[rr % 4] = parts[rr % 4] + buf[slot, rr]
            acc = (parts[0] + parts[1]) + (parts[2] + parts[3])
            if t % PAGES_PER_BLOCK == 0:
                blk_sum = acc
            else:
                blk_sum = blk_sum + acc
                sums.append(blk_sum * (1.0 / MOBA_BLOCK))
        base = pl.multiple_of(gi * BLOCKS_PER_GROUP, BLOCKS_PER_GROUP)
        for hh in range(N_HEADS):
            tile = jnp.zeros((BLOCKS_PER_GROUP, HEAD_DIM), F32)
            for tb in range(BLOCKS_PER_GROUP):
                row = jnp.broadcast_to(sums[tb][hh:hh + 1, :], (BLOCKS_PER_GROUP, HEAD_DIM))
                tile = jnp.where(sub == tb, row, tile)
            out_ref[0, 0, hh, pl.ds(base, BLOCKS_PER_GROUP), :] = tile
        return carry

    lax.fori_loop(0, n_pages // pages_per_group, group, 0)


def _pagesum(pt_flat, cache_k, n_seq, n_pages):
    depth = cache_k.shape[0]
    nblk = n_pages // PAGES_PER_BLOCK
    return pl.pallas_call(
        functools.partial(_pagesum_kernel, n_pages=n_pages),
        grid_spec=pltpu.PrefetchScalarGridSpec(
            num_scalar_prefetch=1,
            grid=(depth, n_seq),
            in_specs=[pl.BlockSpec(memory_space=pl.ANY)],
            out_specs=pl.BlockSpec((1, 1, N_HEADS, nblk, HEAD_DIM), lambda l, b, pt: (l, b, 0, 0, 0)),
            scratch_shapes=[
                pltpu.VMEM((PAGE_RING, PAGE_SIZE, N_HEADS, HEAD_DIM), F32),
                pltpu.SemaphoreType.DMA((PAGE_RING,)),
            ],
        ),
        out_shape=jax.ShapeDtypeStruct((depth, n_seq, N_HEADS, nblk, HEAD_DIM), F32),
        compiler_params=_cparams(("arbitrary", "arbitrary")),
        name="pagesum",
    )(pt_flat, cache_k)


def _gate_kernel(km_ref, q_ref, sel_ref):
    lq = q_ref.shape[0]
    nblk = km_ref.shape[3]
    ncol = N_HEADS * lq
    rows = lax.broadcasted_iota(I32, (ncol, HEAD_DIM), 0)
    g_t = jnp.zeros((nblk, ncol), F32)
    for hh in range(N_HEADS):
        qh = q_ref[:, hh * HEAD_DIM:(hh + 1) * HEAD_DIM]
        qpad = jnp.concatenate([qh] * N_HEADS, axis=0)
        qpad = jnp.where((rows >= hh * lq) & (rows < (hh + 1) * lq), qpad, 0.0)
        g_t = g_t + lax.dot_general(km_ref[0, 0, hh], qpad, NT_DIMS, precision=HIGHEST,
                                    preferred_element_type=F32)
    valid = jnp.full(g_t.shape, True)
    _, rank = _topk_mask(g_t, valid)
    bidx = lax.broadcasted_iota(I32, g_t.shape, 0).astype(F32)
    out = []
    for r in range(MOBA_TOPK):
        out.append(jnp.sum(jnp.where(rank == r, bidx, 0.0), axis=0, keepdims=True))
    out.append(jnp.zeros((8 - MOBA_TOPK, ncol), F32))
    sel_ref[0] = jnp.concatenate(out, axis=0).astype(I32)


def _gate(kmean_all, q, layer, n_seq, lq):
    nblk = kmean_all.shape[3]
    ncol = N_HEADS * lq
    return pl.pallas_call(
        _gate_kernel,
        grid=(n_seq,),
        in_specs=[
            pl.BlockSpec((1, 1, N_HEADS, nblk, HEAD_DIM), lambda b: (layer, b, 0, 0, 0)),
            pl.BlockSpec((lq, N_HEADS * HEAD_DIM), lambda b: (b, 0)),
        ],
        out_specs=pl.BlockSpec((1, 8, ncol), lambda b: (b, 0, 0)),
        out_shape=jax.ShapeDtypeStruct((n_seq, 8, ncol), I32),
        compiler_params=_cparams(("arbitrary",)),
        name="gate",
    )(kmean_all, q)


def _sattn_kernel(pt_ref, sel_ref, q_ref, kn_ref, vn_ref, ck_hbm, cv_hbm, o_ref, kbuf, vbuf, sem,
                  *, layer, n_pages, lq):
    s = pl.program_id(0)
    nsteps = pl.num_programs(0)
    nseg = lq * MOBA_TOPK
    seg_rows = MOBA_TOPK * MOBA_BLOCK

    def copies(step, slot):
        b = lax.shift_right_logical(step, 3)
        hh = step & (N_HEADS - 1)
        out = []
        for seg in range(nseg):
            blk = sel_ref[step * nseg + seg]
            for half in range(PAGES_PER_BLOCK):
                page = pt_ref[b * n_pages + blk * PAGES_PER_BLOCK + half]
                dst = pl.ds(seg * MOBA_BLOCK + half * PAGE_SIZE, PAGE_SIZE)
                out.append(pltpu.make_async_copy(ck_hbm.at[layer, page, :, hh, :], kbuf.at[slot, dst, :], sem.at[0, slot]))
                out.append(pltpu.make_async_copy(cv_hbm.at[layer, page, :, hh, :], vbuf.at[slot, dst, :], sem.at[1, slot]))
        return out

    @pl.when(s == 0)
    def _():
        for cp in copies(s, 0):
            cp.start()

    @pl.when(s + 1 < nsteps)
    def _():
        for cp in copies(s + 1, (s + 1) & 1):
            cp.start()

    slot = s & 1
    for cp in copies(s, slot):
        cp.wait()

    qb = (q_ref[...] * SM_SCALE).astype(BF16)
    s_t = lax.dot_general(kbuf[slot].astype(BF16), qb, NT_DIMS, preferred_element_type=F32)
    key = lax.broadcasted_iota(I32, s_t.shape, 0)
    col = lax.broadcasted_iota(I32, s_t.shape, 1)
    mine = (key >= col * seg_rows) & (key < (col + 1) * seg_rows)
    s_t = jnp.where(mine, s_t, NEG)
    sn_t = lax.dot_general(kn_ref[...].astype(BF16), qb, NT_DIMS, preferred_element_type=F32)
    rn = lax.broadcasted_iota(I32, sn_t.shape, 0)
    cn = lax.broadcasted_iota(I32, sn_t.shape, 1)
    sn_t = jnp.where(rn <= cn, sn_t, NEG)
    m = jnp.maximum(jnp.max(s_t, axis=0, keepdims=True), jnp.max(sn_t, axis=0, keepdims=True))
    p = jnp.exp(s_t - m)
    pn = jnp.exp(sn_t - m)
    inv_l = 1.0 / (jnp.sum(p, axis=0, keepdims=True) + jnp.sum(pn, axis=0, keepdims=True))
    p = p * inv_l
    pn = pn * inv_l
    vn = vn_ref[...]
    rows = []
    for qq in range(lq):
        seg = slice(qq * seg_rows, (qq + 1) * seg_rows)
        o = jnp.sum(p[seg, qq:qq + 1] * vbuf[slot, seg, :], axis=0, keepdims=True)
        o = o + jnp.sum(pn[:, qq:qq + 1] * vn, axis=0, keepdims=True)
        rows.append(o)
    o_ref[...] = jnp.concatenate(rows, axis=0)


def _sattn(pt_flat, sel_flat, q, kn, vn, cache_k, cache_v, layer, n_seq, lq, n_pages):
    nseg = lq * MOBA_TOPK
    blk = lambda s, pt, sel: (lax.shift_right_logical(s, 3), s & (N_HEADS - 1))
    spec = pl.BlockSpec((lq, HEAD_DIM), blk)
    return pl.pallas_call(
        functools.partial(_sattn_kernel, layer=layer, n_pages=n_pages, lq=lq),
        grid_spec=pltpu.PrefetchScalarGridSpec(
            num_scalar_prefetch=2,
            grid=(n_seq * N_HEADS,),
            in_specs=[spec, spec, spec, pl.BlockSpec(memory_space=pl.ANY), pl.BlockSpec(memory_space=pl.ANY)],
            out_specs=spec,
            scratch_shapes=[
                pltpu.VMEM((2, nseg * MOBA_BLOCK, HEAD_DIM), F32),
                pltpu.VMEM((2, nseg * MOBA_BLOCK, HEAD_DIM), F32),
                pltpu.SemaphoreType.DMA((2, 2)),
            ],
        ),
        out_shape=jax.ShapeDtypeStruct(q.shape, F32),
        compiler_params=_cparams(("arbitrary",)),
        name="sattn",
    )(pt_flat, sel_flat, q, kn, vn, cache_k, cache_v)


def _merge_kernel(h_ref, wg1_ref, wg2_ref, py_ref, wpp_ref, at_ref, wap_ref, o_ref):
    h = h_ref[...]
    g1 = jax.nn.sigmoid(jnp.dot(h, wg1_ref[...], preferred_element_type=F32))
    g2 = jax.nn.sigmoid(jnp.dot(h, wg2_ref[...], preferred_element_type=F32))
    a = jnp.dot(py_ref[...], wpp_ref[...], preferred_element_type=F32)
    b = jnp.dot(at_ref[...], wap_ref[...], preferred_element_type=F32)
    o_ref[...] = (g1 * a + g2 * b).astype(BF16)


def _merge(h, wgate, py, wpp, at, wap, layer, tm, tn):
    t, d = h.shape
    dp = py.shape[1]
    da = at.shape[1]
    nj = d // tn
    row = lambda i, j: (i, 0)
    return pl.pallas_call(
        _merge_kernel,
        grid=(t // tm, nj),
        in_specs=[
            pl.BlockSpec((tm, d), row),
            pl.BlockSpec((None, d, tn), lambda i, j: (layer, 0, j)),
            pl.BlockSpec((None, d, tn), lambda i, j: (layer, 0, j + nj)),
            pl.BlockSpec((tm, dp), row),
            pl.BlockSpec((None, dp, tn), lambda i, j: (layer, 0, j)),
            pl.BlockSpec((tm, da), row),
            pl.BlockSpec((None, da, tn), lambda i, j: (layer, 0, j)),
        ],
        out_specs=pl.BlockSpec((tm, tn), lambda i, j: (i, j)),
        out_shape=jax.ShapeDtypeStruct((t, d), BF16),
        compiler_params=_cparams(("arbitrary", "arbitrary")),
        name="merge",
    )(h, wgate, wgate, py, wpp, at, wap)


def _resproj_kernel(a_ref, w_ref, x_ref, o_ref):
    k = pl.program_id(2)
    acc = jnp.dot(a_ref[...], w_ref[...], preferred_element_type=F32)

    @pl.when(k == 0)
    def _():
        o_ref[...] = x_ref[...] + acc

    @pl.when(k > 0)
    def _():
        o_ref[...] = o_ref[...] + acc


def _resproj(a, w, x, layer, tm, tn, tk):
    t, kd = a.shape
    n = w.shape[2]
    return pl.pallas_call(
        _resproj_kernel,
        grid=(t // tm, n // tn, kd // tk),
        in_specs=[
            pl.BlockSpec((tm, tk), lambda i, j, k: (i, k)),
            pl.BlockSpec((None, tk, tn), lambda i, j, k: (layer, k, j)),
            pl.BlockSpec((tm, tn), lambda i, j, k: (i, j)),
        ],
        out_specs=pl.BlockSpec((tm, tn), lambda i, j, k: (i, j)),
        out_shape=jax.ShapeDtypeStruct((t, n), F32),
        compiler_params=_cparams(("arbitrary", "arbitrary", "arbitrary")),
        name="resproj",
    )(a, w, x)


def _mlp_up_kernel(x_ref, g_ref, w_ref, o_ref, h2):
    @pl.when(pl.program_id(1) == 0)
    def _():
        h2[...] = _rms(x_ref[...], g_ref[...]).astype(BF16)

    acc = jnp.dot(h2[...], w_ref[...], preferred_element_type=F32)
    o_ref[...] = jnp.square(jnp.maximum(acc, 0.0)).astype(BF16)


def _mlp_up(x, g, w, layer, tm, tn):
    t, d = x.shape
    n = w.shape[2]
    return pl.pallas_call(
        _mlp_up_kernel,
        grid=(t // tm, n // tn),
        in_specs=[
            pl.BlockSpec((tm, d), lambda i, j: (i, 0)),
            pl.BlockSpec((None, 1, d), lambda i, j: (layer, 0, 0)),
            pl.BlockSpec((None, d, tn), lambda i, j: (layer, 0, j)),
        ],
        out_specs=pl.BlockSpec((tm, tn), lambda i, j: (i, j)),
        out_shape=jax.ShapeDtypeStruct((t, n), BF16),
        scratch_shapes=[pltpu.VMEM((tm, d), BF16)],
        compiler_params=_cparams(("arbitrary", "arbitrary")),
        name="mlp_up",
    )(x, g, w)


def _final_norm_kernel(x_ref, g_ref, o_ref):
    o_ref[...] = _rms(x_ref[...], g_ref[...])


def _final_norm(x, g, tm):
    t, d = x.shape
    return pl.pallas_call(
        _final_norm_kernel,
        grid=(t // tm,),
        in_specs=[pl.BlockSpec((tm, d), lambda i: (i, 0)), pl.BlockSpec((1, d), lambda i: (0, 0))],
        out_specs=pl.BlockSpec((tm, d), lambda i: (i, 0)),
        out_shape=jax.ShapeDtypeStruct((t, d), F32),
        compiler_params=_cparams(("arbitrary",)),
        name="final_norm",
    )(x, g)


def _rope_tables(pos):
    inv = ROPE_THETA ** (-np.arange(0, HEAD_DIM, 2, dtype=np.float64) / HEAD_DIM)
    ang = np.asarray(pos, np.float64)[:, None] * inv[None, :]
    cos = np.concatenate([np.cos(ang), np.cos(ang)], axis=1)
    sin = np.concatenate([-np.sin(ang), np.sin(ang)], axis=1)
    return jnp.asarray(cos, F32), jnp.asarray(sin, F32)


def _row_layers(x, h, py, at, weights, layer, tm):
    wgate, wpp, wap, wout, gmlp, wup, wdown = weights
    d = x.shape[1]
    tn = min(512, d)
    merged = _merge(h, wgate, py, wpp, at, wap, layer, tm, tn)
    x1 = _resproj(merged, wout, x, layer, tm, min(1024, d), d)
    a = _mlp_up(x1, gmlp, wup, layer, tm, 1024)
    return _resproj(a, wdown, x1, layer, tm, min(1024, d), 2048)


def kernel(x_prompt, x_sample, cache_k, cache_v, state_pool, page_table, g_mix, w_in, w_pool_grp, pool_scale, w_pool_proj, w_att_proj, w_gate, w_out, g_mlp, w_up, w_down, g_final):
    bp, sp, d = x_prompt.shape
    bs, ls, _ = x_sample.shape
    depth = w_in.shape[0]
    n_pages = page_table.shape[1]
    past = n_pages * PAGE_SIZE
    dp = pool_scale.shape[1]
    tp = 512
    ts_rows = bs * ls

    xp = x_prompt.reshape(bp * sp, d)
    xs = x_sample.reshape(ts_rows, d)
    cos_p, sin_p = _rope_tables(np.arange(sp))
    cos_s, sin_s = _rope_tables(np.tile(past + np.arange(ls), bs))
    pt_flat = page_table.reshape(-1)
    kmean_all = _pagesum(pt_flat, cache_k, bs, n_pages)
    prev_p = jnp.zeros((bp, POOL_HALO, dp), F32)

    gm = g_mix.reshape(depth, 1, d)
    win = w_in.astype(BF16)
    wgrp = w_pool_grp.astype(BF16)
    psc = pool_scale.reshape(depth, 1, dp)
    rest = (w_gate.astype(BF16), w_pool_proj.astype(BF16), w_att_proj.astype(BF16),
            w_out.astype(BF16), g_mlp.reshape(depth, 1, d), w_up.astype(BF16), w_down.astype(BF16))

    kp_l, vp_l, pp_l, ks_l, vs_l, ps_l = [], [], [], [], [], []
    for l in range(depth):
        hp, up, qp, kp, vp, kbp, vbp = _inproj(xp, gm, win, cos_p, sin_p, l, tp)
        up3 = up.reshape(bp, sp, dp)
        pyp = _pool(up3, prev_p, wgrp, psc, l, tp, 0).reshape(bp * sp, dp)
        atp = _attn_prompt(qp, kbp, vbp, _kmean(kp), bp, sp)
        xp = _row_layers(xp, hp, pyp, atp, rest, l, tp)
        kp_l.append(kp.reshape(bp, sp // PAGE_SIZE, PAGE_SIZE, N_HEADS, HEAD_DIM))
        vp_l.append(vp.reshape(bp, sp // PAGE_SIZE, PAGE_SIZE, N_HEADS, HEAD_DIM))
        pp_l.append(up3[:, sp - POOL_STATE:, :])

        hs, us, qs, ks, vs, _, _ = _inproj(xs, gm, win, cos_s, sin_s, l, ts_rows)
        us3 = us.reshape(bs, ls, dp)
        prev_s = jnp.pad(state_pool[l], ((0, 0), (POOL_HALO - POOL_STATE, 0), (0, 0)))
        pys = _pool(us3, prev_s, wgrp, psc, l, ls, past).reshape(ts_rows, dp)
        sel = _gate(kmean_all, qs, l, bs, ls)
        sel_flat = sel[:, :MOBA_TOPK, :].transpose(0, 2, 1).reshape(-1)
        ats = _sattn(pt_flat, sel_flat, qs, ks, vs, cache_k, cache_v, l, bs, ls, n_pages).astype(BF16)
        xs = _row_layers(xs, hs, pys, ats, rest, l, ts_rows)
        ks_l.append(ks.reshape(bs, ls, N_HEADS, HEAD_DIM))
        vs_l.append(vs.reshape(bs, ls, N_HEADS, HEAD_DIM))
        ps_l.append(jnp.concatenate([state_pool[l], us3], axis=1)[:, -POOL_STATE:, :])

    gf = g_final.reshape(1, d)
    y_prompt = _final_norm(xp, gf, tp).reshape(bp, sp, d)
    y_sample = _final_norm(xs, gf, ts_rows).reshape(bs, ls, d)
    return (y_prompt, y_sample, jnp.stack(kp_l), jnp.stack(vp_l), jnp.stack(pp_l),
            jnp.stack(ks_l), jnp.stack(vs_l), jnp.stack(ps_l))
```

```python
import functools

import numpy as np
import jax
import jax.numpy as jnp
from jax import lax
from jax.experimental import pallas as pl
from jax.experimental.pallas import tpu as pltpu

F32 = jnp.float32
BF16 = jnp.bfloat16
I32 = jnp.int32
HIGHEST = lax.Precision.HIGHEST

N_HEADS = 8
HEAD_DIM = 128
POOL_WINDOWS = (2, 4, 8, 16)
POOL_GROUP = 256
POOL_STATE = 15
POOL_HALO = 16
MOBA_BLOCK = 256
MOBA_TOPK = 3
PAGE_SIZE = 128
PAGES_PER_BLOCK = MOBA_BLOCK // PAGE_SIZE
ROPE_THETA = 10000.0
EPS = 1e-6
NEG = -1e30
SM_SCALE = HEAD_DIM ** -0.5
LOG2E = 1.4426950408889634

VMEM_LIMIT = 52 * 1024 * 1024
ROW_TILE = 256
POOL_TILE = 512
NT_DIMS = (((1,), (1,)), ((), ()))


def _cparams(sem):
    return pltpu.CompilerParams(dimension_semantics=sem, vmem_limit_bytes=VMEM_LIMIT)


def _rms(x, g):
    ms = jnp.mean(x * x, axis=-1, keepdims=True)
    return x * lax.rsqrt(ms + EPS) * g


def _inproj_kernel(x_ref, g_ref, w_ref, cos_ref, sin_ref, h_ref, u_ref, q_ref, k_ref, v_ref, kb_ref, vb_ref):
    n = u_ref.shape[1]
    h = _rms(x_ref[...], g_ref[...]).astype(BF16)
    h_ref[...] = h

    def rope(a):
        c = cos_ref[...]
        s = sin_ref[...]
        parts = []
        for hh in range(N_HEADS):
            ah = a[:, hh * HEAD_DIM:(hh + 1) * HEAD_DIM]
            parts.append(ah * c + pltpu.roll(ah, HEAD_DIM // 2, axis=1) * s)
        return jnp.concatenate(parts, axis=1)

    proj = lambda part: jnp.dot(h, w_ref[:, part * n:(part + 1) * n], preferred_element_type=F32)
    u_ref[...] = proj(0)
    q_ref[...] = rope(proj(1))
    k = rope(proj(2))
    k_ref[...] = k
    kb_ref[...] = k.astype(BF16)
    v = proj(3)
    v_ref[...] = v
    vb_ref[...] = v.astype(BF16)


def _resident(block_shape, index_map):
    return pl.BlockSpec(block_shape, index_map, pipeline_mode=pl.Buffered(1))


def _inproj(x, g, w, cos, sin, layer, tm):
    t, d = x.shape
    n = w.shape[2] // 4
    pos_blocks = cos.shape[0] // tm
    row = lambda i: (i, 0)
    tab = lambda i: (i % pos_blocks, 0)
    return pl.pallas_call(
        _inproj_kernel,
        grid=(t // tm,),
        in_specs=[
            pl.BlockSpec((tm, d), row),
            _resident((None, 1, d), lambda i: (layer, 0, 0)),
            _resident((None, d, 4 * n), lambda i: (layer, 0, 0)),
            pl.BlockSpec((tm, HEAD_DIM), tab),
            pl.BlockSpec((tm, HEAD_DIM), tab),
        ],
        out_specs=[pl.BlockSpec((tm, d), row)] + [pl.BlockSpec((tm, n), row)] * 6,
        out_shape=[jax.ShapeDtypeStruct((t, d), BF16)]
        + [jax.ShapeDtypeStruct((t, n), F32)] * 4
        + [jax.ShapeDtypeStruct((t, n), BF16)] * 2,
        compiler_params=_cparams(("arbitrary",)),
        name="inproj",
    )(x, g, w, cos, sin)


def _pool_kernel(*refs, ts, n_tiles, start):
    if n_tiles > 1:
        u_ref, halo_ref, prev_ref, wg_ref, sc_ref, y_ref, z = refs
    else:
        u_ref, prev_ref, wg_ref, sc_ref, y_ref, z = refs
    j = pl.program_id(1)
    z[POOL_HALO:POOL_HALO + ts, :] = u_ref[0]
    if n_tiles > 1:
        @pl.when(j == 0)
        def _():
            z[0:POOL_HALO, :] = prev_ref[0]

        @pl.when(j > 0)
        def _():
            z[0:POOL_HALO, :] = halo_ref[0]
    else:
        z[0:POOL_HALO, :] = prev_ref[0]

    pos = start + j * ts + lax.broadcasted_iota(I32, (ts, 1), 0)
    for g, w in enumerate(POOL_WINDOWS):
        cols = slice(g * POOL_GROUP, (g + 1) * POOL_GROUP)
        u = z[POOL_HALO:POOL_HALO + ts, cols]
        s = u
        for i in range(1, w):
            s = s + z[POOL_HALO - i:POOL_HALO - i + ts, cols]
        cnt = jnp.minimum(w, pos + 1).astype(F32)
        p = s / cnt - u
        y = jnp.dot(p.astype(BF16), wg_ref[g], preferred_element_type=F32) * sc_ref[:, cols]
        y_ref[0, :, cols] = y.astype(BF16)


def _pool(u, prev16, wg, scale, layer, ts, start):
    b, s, dp = u.shape
    n_tiles = s // ts
    in_specs = [pl.BlockSpec((1, ts, dp), lambda bi, j: (bi, j, 0))]
    args = [u]
    if n_tiles > 1:
        hb = ts // POOL_HALO
        in_specs.append(pl.BlockSpec((1, POOL_HALO, dp), lambda bi, j: (bi, jnp.maximum(j * hb - 1, 0), 0)))
        args.append(u)
    in_specs += [
        pl.BlockSpec((1, POOL_HALO, dp), lambda bi, j: (bi, 0, 0)),
        pl.BlockSpec((None,) + wg.shape[1:], lambda bi, j: (layer, 0, 0, 0)),
        pl.BlockSpec((None, 1, dp), lambda bi, j: (layer, 0, 0)),
    ]
    args += [prev16, wg, scale]
    return pl.pallas_call(
        functools.partial(_pool_kernel, ts=ts, n_tiles=n_tiles, start=start),
        grid=(b, n_tiles),
        in_specs=in_specs,
        out_specs=pl.BlockSpec((1, ts, dp), lambda bi, j: (bi, j, 0)),
        out_shape=jax.ShapeDtypeStruct((b, s, dp), BF16),
        scratch_shapes=[pltpu.VMEM((POOL_HALO + ts, dp), F32)],
        compiler_params=_cparams(("arbitrary", "arbitrary")),
        name="pool",
    )(*args)


def _topk_mask(g_t, valid):
    nb = g_t.shape[0]
    jidx = lax.broadcasted_iota(I32, g_t.shape, 0)
    g = jnp.where(valid, g_t, -jnp.inf)
    rank = jnp.zeros(g_t.shape, I32)
    for j in range(nb):
        row = g[j:j + 1, :]
        beats = (row > g) | ((row == g) & (jidx > j))
        rank = rank + beats.astype(I32)
    return valid & (rank < MOBA_TOPK), rank


ATTN_HEADS_PER_STEP = 2
ATTN_CLASS_BLOCKS = 4
KMEAN_BLOCKS_PER_STEP = 8


def _attn_prompt_kernel(q_ref, kb_ref, vb_ref, km_ref, o_ref, vt, s_scr, p_scr, *, hg, nb):
    i = pl.program_id(2)
    blk = MOBA_BLOCK
    hcols = lambda hh: slice(hh * HEAD_DIM, (hh + 1) * HEAD_DIM)

    @pl.when(i == 0)
    def _():
        for hh in range(hg):
            for j in range(nb):
                vblk = vb_ref[j * blk:(j + 1) * blk, hcols(hh)]
                vt[hh, :, j * blk:(j + 1) * blk] = vblk.astype(F32).T.astype(BF16)

    rc = lax.broadcasted_iota(I32, (blk, blk), 0) - lax.broadcasted_iota(I32, (blk, blk), 1)
    jidx = lax.broadcasted_iota(I32, (nb, blk), 0)

    def attend(hh, n, bias, qs):
        m = None
        for j in range(n):
            s = jnp.dot(kb_ref[j * blk:(j + 1) * blk, hcols(hh)], qs, preferred_element_type=F32)
            s = s + bias[j:j + 1, :]
            if j >= n - ATTN_CLASS_BLOCKS:
                s = jnp.where(rc <= jnp.where(i == j, 0, blk), s, NEG)
            s_scr[hh, j * blk:(j + 1) * blk, :] = s
            cm = jnp.max(s, axis=0, keepdims=True)
            m = cm if m is None else jnp.maximum(m, cm)
        l = jnp.zeros_like(m)
        for j in range(n):
            p = jnp.exp2(s_scr[hh, j * blk:(j + 1) * blk, :] - m)
            l = l + jnp.sum(p, axis=0, keepdims=True)
            p_scr[hh, j * blk:(j + 1) * blk, :] = p.astype(BF16)
        acc = jnp.dot(vt[hh, :, 0:n * blk], p_scr[hh, 0:n * blk, :], preferred_element_type=F32)
        o_ref[:, hcols(hh)] = (acc / l).T.astype(BF16)

    for hh in range(hg):
        q_t = q_ref[:, hcols(hh)].T
        g_t = jnp.dot(km_ref[:, hcols(hh)], q_t, precision=HIGHEST, preferred_element_type=F32)
        sel, _ = _topk_mask(g_t, jidx < i)
        bias = jnp.where(sel | (jidx == i), 0.0, NEG)
        qs = (q_t * (SM_SCALE * LOG2E)).astype(BF16)
        for n in range(ATTN_CLASS_BLOCKS, nb + 1, ATTN_CLASS_BLOCKS):
            @pl.when((i >= n - ATTN_CLASS_BLOCKS) & (i < n))
            def _(hh=hh, n=n, bias=bias, qs=qs):
                attend(hh, n, bias, qs)


def _attn_prompt(q, kb, vb, kmean, batch, seq):
    t, da = q.shape
    nb = seq // MOBA_BLOCK
    hg = ATTN_HEADS_PER_STEP
    wcols = hg * HEAD_DIM
    qmap = lambda b, g, i: (b * nb + i, g)
    kvmap = lambda b, g, i: (b, g)
    return pl.pallas_call(
        functools.partial(_attn_prompt_kernel, hg=hg, nb=nb),
        grid=(batch, N_HEADS // hg, nb),
        in_specs=[
            pl.BlockSpec((MOBA_BLOCK, wcols), qmap),
            pl.BlockSpec((seq, wcols), kvmap),
            pl.BlockSpec((seq, wcols), kvmap),
            pl.BlockSpec((nb, wcols), kvmap),
        ],
        out_specs=pl.BlockSpec((MOBA_BLOCK, wcols), qmap),
        out_shape=jax.ShapeDtypeStruct((t, da), BF16),
        scratch_shapes=[
            pltpu.VMEM((hg, HEAD_DIM, seq), BF16),
            pltpu.VMEM((hg, seq, MOBA_BLOCK), F32),
            pltpu.VMEM((hg, seq, MOBA_BLOCK), BF16),
        ],
        compiler_params=_cparams(("arbitrary", "arbitrary", "arbitrary")),
        name="attn_prompt",
    )(q, kb, vb, kmean)


def _kmean_kernel(k_ref, o_ref):
    rows = []
    for j in range(KMEAN_BLOCKS_PER_STEP):
        rows.append(jnp.sum(k_ref[j * MOBA_BLOCK:(j + 1) * MOBA_BLOCK, :], axis=0, keepdims=True))
    o_ref[...] = jnp.concatenate(rows, axis=0) * (1.0 / MOBA_BLOCK)


def _kmean(k):
    t, da = k.shape
    rows = KMEAN_BLOCKS_PER_STEP * MOBA_BLOCK
    return pl.pallas_call(
        _kmean_kernel,
        grid=(t // rows,),
        in_specs=[pl.BlockSpec((rows, da), lambda i: (i, 0))],
        out_specs=pl.BlockSpec((KMEAN_BLOCKS_PER_STEP, da), lambda i: (i, 0)),
        out_shape=jax.ShapeDtypeStruct((t // MOBA_BLOCK, da), F32),
        compiler_params=_cparams(("arbitrary",)),
        name="kmean",
    )(k)


PAGE_RING = 16
BLOCKS_PER_GROUP = 8


def _pagesum_kernel(pt_ref, ck_hbm, out_ref, buf, sem, *, n_pages):
    l = pl.program_id(0)
    b = pl.program_id(1)
    pages_per_group = BLOCKS_PER_GROUP * PAGES_PER_BLOCK
    assert pages_per_group % PAGE_RING == 0

    def copy(p, slot):
        return pltpu.make_async_copy(ck_hbm.at[l, pt_ref[b * n_pages + p]], buf.at[slot], sem.at[slot])

    for p in range(PAGE_RING - 1):
        copy(p, p).start()

    sub = lax.broadcasted_iota(I32, (BLOCKS_PER_GROUP, HEAD_DIM), 0)

    def group(gi, carry):
        sums = []
        for t in range(pages_per_group):
            p = gi * pages_per_group + t
            nxt = p + PAGE_RING - 1

            @pl.when(nxt < n_pages)
            def _():
                copy(nxt, (t + PAGE_RING - 1) % PAGE_RING).start()

            slot = t % PAGE_RING
            copy(p, slot).wait()
            parts = [buf[slot, rr] for rr in range(4)]
            for rr in range(4, PAGE_SIZE):
                parts[rr % 4] = parts[rr % 4] + buf[slot, rr]
            acc = (parts[0] + parts[1]) + (parts[2] + parts[3])
            if t % PAGES_PER_BLOCK == 0:
                blk_sum = acc
            else:
                blk_sum = blk_sum + acc
                sums.append(blk_sum * (1.0 / MOBA_BLOCK))
        base = pl.multiple_of(gi * BLOCKS_PER_GROUP, BLOCKS_PER_GROUP)
        for hh in range(N_HEADS):
            tile = jnp.zeros((BLOCKS_PER_GROUP, HEAD_DIM), F32)
            for tb in range(BLOCKS_PER_GROUP):
                row = jnp.broadcast_to(sums[tb][hh:hh + 1, :], (BLOCKS_PER_GROUP, HEAD_DIM))
                tile = jnp.where(sub == tb, row, tile)
            out_ref[0, 0, hh, pl.ds(base, BLOCKS_PER_GROUP), :] = tile
        return carry

    lax.fori_loop(0, n_pages // pages_per_group, group, 0)


def _pagesum(pt_flat, cache_k, n_seq, n_pages):
    depth = cache_k.shape[0]
    nblk = n_pages // PAGES_PER_BLOCK
    return pl.pallas_call(
        functools.partial(_pagesum_kernel, n_pages=n_pages),
        grid_spec=pltpu.PrefetchScalarGridSpec(
            num_scalar_prefetch=1,
            grid=(depth, n_seq),
            in_specs=[pl.BlockSpec(memory_space=pl.ANY)],
            out_specs=pl.BlockSpec((1, 1, N_HEADS, nblk, HEAD_DIM), lambda l, b, pt: (l, b, 0, 0, 0)),
            scratch_shapes=[
                pltpu.VMEM((PAGE_RING, PAGE_SIZE, N_HEADS, HEAD_DIM), F32),
                pltpu.SemaphoreType.DMA((PAGE_RING,)),
            ],
        ),
        out_shape=jax.ShapeDtypeStruct((depth, n_seq, N_HEADS, nblk, HEAD_DIM), F32),
        compiler_params=_cparams(("arbitrary", "arbitrary")),
        name="pagesum",
    )(pt_flat, cache_k)


def _gate_kernel(km_ref, q_ref, sel_ref):
    lq = q_ref.shape[0]
    nblk = km_ref.shape[3]
    ncol = N_HEADS * lq
    rows = lax.broadcasted_iota(I32, (ncol, HEAD_DIM), 0)
    g_t = jnp.zeros((nblk, ncol), F32)
    for hh in range(N_HEADS):
        qh = q_ref[:, hh * HEAD_DIM:(hh + 1) * HEAD_DIM]
        qpad = jnp.concatenate([qh] * N_HEADS, axis=0)
        qpad = jnp.where((rows >= hh * lq) & (rows < (hh + 1) * lq), qpad, 0.0)
        g_t = g_t + lax.dot_general(km_ref[0, 0, hh], qpad, NT_DIMS, precision=HIGHEST,
                                    preferred_element_type=F32)
    valid = jnp.full(g_t.shape, True)
    _, rank = _topk_mask(g_t, valid)
    bidx = lax.broadcasted_iota(I32, g_t.shape, 0).astype(F32)
    out = []
    for r in range(MOBA_TOPK):
        out.append(jnp.sum(jnp.where(rank == r, bidx, 0.0), axis=0, keepdims=True))
    out.append(jnp.zeros((8 - MOBA_TOPK, ncol), F32))
    sel_ref[0] = jnp.concatenate(out, axis=0).astype(I32)


def _gate(kmean_all, q, layer, n_seq, lq):
    nblk = kmean_all.shape[3]
    ncol = N_HEADS * lq
    return pl.pallas_call(
        _gate_kernel,
        grid=(n_seq,),
        in_specs=[
            pl.BlockSpec((1, 1, N_HEADS, nblk, HEAD_DIM), lambda b: (layer, b, 0, 0, 0)),
            pl.BlockSpec((lq, N_HEADS * HEAD_DIM), lambda b: (b, 0)),
        ],
        out_specs=pl.BlockSpec((1, 8, ncol), lambda b: (b, 0, 0)),
        out_shape=jax.ShapeDtypeStruct((n_seq, 8, ncol), I32),
        compiler_params=_cparams(("arbitrary",)),
        name="gate",
    )(kmean_all, q)


def _sattn_kernel(pt_ref, sel_ref, q_ref, kn_ref, vn_ref, ck_hbm, cv_hbm, o_ref, kbuf, vbuf, sem,
                  *, layer, n_pages, lq):
    s = pl.program_id(0)
    nsteps = pl.num_programs(0)
    nseg = lq * MOBA_TOPK

    def copies(step, slot):
        b = lax.shift_right_logical(step, 3)
        hh = step & (N_HEADS - 1)
        out = []
        for qq in range(lq):
            lanes = pl.ds(qq * HEAD_DIM, HEAD_DIM)
            for r in range(MOBA_TOPK):
                blk = sel_ref[step * nseg + qq * MOBA_TOPK + r]
                for half in range(PAGES_PER_BLOCK):
                    page = pt_ref[b * n_pages + blk * PAGES_PER_BLOCK + half]
                    rows = pl.ds(r * MOBA_BLOCK + half * PAGE_SIZE, PAGE_SIZE)
                    out.append(pltpu.make_async_copy(ck_hbm.at[layer, page, :, hh, :], kbuf.at[slot, rows, lanes], sem.at[0, slot]))
                    out.append(pltpu.make_async_copy(cv_hbm.at[layer, page, :, hh, :], vbuf.at[slot, rows, lanes], sem.at[1, slot]))
        return out

    @pl.when(s == 0)
    def _():
        for cp in copies(s, 0):
            cp.start()

    @pl.when(s + 1 < nsteps)
    def _():
        for cp in copies(s + 1, (s + 1) & 1):
            cp.start()

    slot = s & 1
    for cp in copies(s, slot):
        cp.wait()

    qb = (q_ref[...] * SM_SCALE).astype(BF16)
    q_bd = jnp.concatenate([qb] * lq, axis=1)
    rq = lax.broadcasted_iota(I32, q_bd.shape, 0)
    cq = lax.broadcasted_iota(I32, q_bd.shape, 1)
    q_bd = jnp.where((cq >= rq * HEAD_DIM) & (cq < (rq + 1) * HEAD_DIM), q_bd, jnp.zeros_like(q_bd))
    s_t = lax.dot_general(kbuf[slot].astype(BF16), q_bd, NT_DIMS, preferred_element_type=F32)
    sn_t = lax.dot_general(kn_ref[...].astype(BF16), qb, NT_DIMS, preferred_element_type=F32)
    rn = lax.broadcasted_iota(I32, sn_t.shape, 0)
    cn = lax.broadcasted_iota(I32, sn_t.shape, 1)
    sn_t = jnp.where(rn <= cn, sn_t, NEG)
    m = jnp.maximum(jnp.max(s_t, axis=0, keepdims=True), jnp.max(sn_t, axis=0, keepdims=True))
    p = jnp.exp(s_t - m)
    pn = jnp.exp(sn_t - m)
    inv_l = 1.0 / (jnp.sum(p, axis=0, keepdims=True) + jnp.sum(pn, axis=0, keepdims=True))
    p = p * inv_l
    pn = pn * inv_l
    vn = vn_ref[...]
    rows = []
    for qq in range(lq):
        lanes = slice(qq * HEAD_DIM, (qq + 1) * HEAD_DIM)
        o = jnp.sum(p[:, qq:qq + 1] * vbuf[slot, :, lanes], axis=0, keepdims=True)
        o = o + jnp.sum(pn[:, qq:qq + 1] * vn, axis=0, keepdims=True)
        rows.append(o)
    o_ref[...] = jnp.concatenate(rows, axis=0)


def _sattn(pt_flat, sel_flat, q, kn, vn, cache_k, cache_v, layer, n_seq, lq, n_pages):
    nseg = lq * MOBA_TOPK
    blk = lambda s, pt, sel: (lax.shift_right_logical(s, 3), s & (N_HEADS - 1))
    spec = pl.BlockSpec((lq, HEAD_DIM), blk)
    return pl.pallas_call(
        functools.partial(_sattn_kernel, layer=layer, n_pages=n_pages, lq=lq),
        grid_spec=pltpu.PrefetchScalarGridSpec(
            num_scalar_prefetch=2,
            grid=(n_seq * N_HEADS,),
            in_specs=[spec, spec, spec, pl.BlockSpec(memory_space=pl.ANY), pl.BlockSpec(memory_space=pl.ANY)],
            out_specs=spec,
            scratch_shapes=[
                pltpu.VMEM((2, MOBA_TOPK * MOBA_BLOCK, lq * HEAD_DIM), F32),
                pltpu.VMEM((2, MOBA_TOPK * MOBA_BLOCK, lq * HEAD_DIM), F32),
                pltpu.SemaphoreType.DMA((2, 2)),
            ],
        ),
        out_shape=jax.ShapeDtypeStruct(q.shape, F32),
        compiler_params=_cparams(("arbitrary",)),
        name="sattn",
    )(pt_flat, sel_flat, q, kn, vn, cache_k, cache_v)


COL_CHUNK = 512


def _merge_kernel(h_ref, wg_ref, py_ref, wpp_ref, at_ref, wap_ref, o_ref):
    d = o_ref.shape[1]
    h = h_ref[...]
    py = py_ref[...]
    at = at_ref[...]
    for c0 in range(0, d, COL_CHUNK):
        cols = slice(c0, c0 + COL_CHUNK)
        g1 = jax.nn.sigmoid(jnp.dot(h, wg_ref[:, cols], preferred_element_type=F32))
        g2 = jax.nn.sigmoid(jnp.dot(h, wg_ref[:, d + c0:d + c0 + COL_CHUNK], preferred_element_type=F32))
        a = jnp.dot(py, wpp_ref[:, cols], preferred_element_type=F32)
        b = jnp.dot(at, wap_ref[:, cols], preferred_element_type=F32)
        o_ref[:, cols] = (g1 * a + g2 * b).astype(BF16)


def _merge(h, wgate, py, wpp, at, wap, layer, tm):
    t, d = h.shape
    dp = py.shape[1]
    da = at.shape[1]
    row = lambda i: (i, 0)
    lay = lambda i: (layer, 0, 0)
    return pl.pallas_call(
        _merge_kernel,
        grid=(t // tm,),
        in_specs=[
            pl.BlockSpec((tm, d), row),
            _resident((None, d, 2 * d), lay),
            pl.BlockSpec((tm, dp), row),
            _resident((None, dp, d), lay),
            pl.BlockSpec((tm, da), row),
            _resident((None, da, d), lay),
        ],
        out_specs=pl.BlockSpec((tm, d), row),
        out_shape=jax.ShapeDtypeStruct((t, d), BF16),
        compiler_params=_cparams(("arbitrary",)),
        name="merge",
    )(h, wgate, py, wpp, at, wap)


def _outproj_kernel(a_ref, w_ref, x_ref, g_ref, x1_ref, h2_ref):
    d = x1_ref.shape[1]
    a = a_ref[...]
    for c0 in range(0, d, COL_CHUNK):
        cols = slice(c0, c0 + COL_CHUNK)
        x1_ref[:, cols] = x_ref[:, cols] + jnp.dot(a, w_ref[:, cols], preferred_element_type=F32)
    h2_ref[...] = _rms(x1_ref[...], g_ref[...]).astype(BF16)


def _outproj(a, w, x, g, layer, tm):
    t, kd = a.shape
    d = w.shape[2]
    row = lambda i: (i, 0)
    return pl.pallas_call(
        _outproj_kernel,
        grid=(t // tm,),
        in_specs=[
            pl.BlockSpec((tm, kd), row),
            _resident((None, kd, d), lambda i: (layer, 0, 0)),
            pl.BlockSpec((tm, d), row),
            _resident((None, 1, d), lambda i: (layer, 0, 0)),
        ],
        out_specs=[pl.BlockSpec((tm, d), row), pl.BlockSpec((tm, d), row)],
        out_shape=[jax.ShapeDtypeStruct((t, d), F32), jax.ShapeDtypeStruct((t, d), BF16)],
        compiler_params=_cparams(("arbitrary",)),
        name="outproj",
    )(a, w, x, g)


def _mlp_up_kernel(h2_ref, w_ref, o_ref):
    n = o_ref.shape[1]
    h2 = h2_ref[...]
    for c0 in range(0, n, COL_CHUNK):
        cols = slice(c0, c0 + COL_CHUNK)
        acc = jnp.dot(h2, w_ref[:, cols], preferred_element_type=F32)
        o_ref[:, cols] = jnp.square(jnp.maximum(acc, 0.0)).astype(BF16)


def _mlp_up(h2, w, layer, tm):
    t, d = h2.shape
    n = w.shape[2]
    row = lambda i: (i, 0)
    return pl.pallas_call(
        _mlp_up_kernel,
        grid=(t // tm,),
        in_specs=[pl.BlockSpec((tm, d), row), _resident((None, d, n), lambda i: (layer, 0, 0))],
        out_specs=pl.BlockSpec((tm, n), row),
        out_shape=jax.ShapeDtypeStruct((t, n), BF16),
        compiler_params=_cparams(("arbitrary",)),
        name="mlp_up",
    )(h2, w)


def _mlp_down_kernel(a_ref, w_ref, x_ref, o_ref):
    d = o_ref.shape[1]
    a = a_ref[...]
    for c0 in range(0, d, COL_CHUNK):
        cols = slice(c0, c0 + COL_CHUNK)
        o_ref[:, cols] = x_ref[:, cols] + jnp.dot(a, w_ref[:, cols], preferred_element_type=F32)


def _mlp_down(a, w, x, layer, tm):
    t, n = a.shape
    d = w.shape[2]
    row = lambda i: (i, 0)
    return pl.pallas_call(
        _mlp_down_kernel,
        grid=(t // tm,),
        in_specs=[
            pl.BlockSpec((tm, n), row),
            _resident((None, n, d), lambda i: (layer, 0, 0)),
            pl.BlockSpec((tm, d), row),
        ],
        out_specs=pl.BlockSpec((tm, d), row),
        out_shape=jax.ShapeDtypeStruct((t, d), F32),
        compiler_params=_cparams(("arbitrary",)),
        name="mlp_down",
    )(a, w, x)


def _final_norm_kernel(x_ref, g_ref, o_ref):
    o_ref[...] = _rms(x_ref[...], g_ref[...])


def _final_norm(x, g, tm):
    t, d = x.shape
    return pl.pallas_call(
        _final_norm_kernel,
        grid=(t // tm,),
        in_specs=[pl.BlockSpec((tm, d), lambda i: (i, 0)), pl.BlockSpec((1, d), lambda i: (0, 0))],
        out_specs=pl.BlockSpec((tm, d), lambda i: (i, 0)),
        out_shape=jax.ShapeDtypeStruct((t, d), F32),
        compiler_params=_cparams(("arbitrary",)),
        name="final_norm",
    )(x, g)


def _rope_tables(pos):
    inv = ROPE_THETA ** (-np.arange(0, HEAD_DIM, 2, dtype=np.float64) / HEAD_DIM)
    ang = np.asarray(pos, np.float64)[:, None] * inv[None, :]
    cos = np.concatenate([np.cos(ang), np.cos(ang)], axis=1)
    sin = np.concatenate([-np.sin(ang), np.sin(ang)], axis=1)
    return jnp.asarray(cos, F32), jnp.asarray(sin, F32)


def _row_layers(x, h, py, at, weights, layer, tm):
    wgate, wpp, wap, wout, gmlp, wup, wdown = weights
    merged = _merge(h, wgate, py, wpp, at, wap, layer, tm)
    x1, h2 = _outproj(merged, wout, x, gmlp, layer, tm)
    a = _mlp_up(h2, wup, layer, tm)
    return _mlp_down(a, wdown, x1, layer, tm)


def kernel(x_prompt, x_sample, cache_k, cache_v, state_pool, page_table, g_mix, w_in, w_pool_grp, pool_scale, w_pool_proj, w_att_proj, w_gate, w_out, g_mlp, w_up, w_down, g_final):
    bp, sp, d = x_prompt.shape
    bs, ls, _ = x_sample.shape
    depth = w_in.shape[0]
    n_pages = page_table.shape[1]
    past = n_pages * PAGE_SIZE
    dp = pool_scale.shape[1]
    tp = ROW_TILE
    ts_rows = bs * ls

    xp = x_prompt.reshape(bp * sp, d)
    xs = x_sample.reshape(ts_rows, d)
    cos_p, sin_p = _rope_tables(np.arange(sp))
    cos_s, sin_s = _rope_tables(np.tile(past + np.arange(ls), bs))
    pt_flat = page_table.reshape(-1)
    kmean_all = _pagesum(pt_flat, cache_k, bs, n_pages)
    prev_p = jnp.zeros((bp, POOL_HALO, dp), F32)

    gm = g_mix.reshape(depth, 1, d)
    win = w_in.astype(BF16)
    wgrp = w_pool_grp.astype(BF16)
    psc = pool_scale.reshape(depth, 1, dp)
    rest = (w_gate.astype(BF16), w_pool_proj.astype(BF16), w_att_proj.astype(BF16),
            w_out.astype(BF16), g_mlp.reshape(depth, 1, d), w_up.astype(BF16), w_down.astype(BF16))

    kp_l, vp_l, pp_l, ks_l, vs_l, ps_l = [], [], [], [], [], []
    for l in range(depth):
        hp, up, qp, kp, vp, kbp, vbp = _inproj(xp, gm, win, cos_p, sin_p, l, tp)
        up3 = up.reshape(bp, sp, dp)
        pyp = _pool(up3, prev_p, wgrp, psc, l, POOL_TILE, 0).reshape(bp * sp, dp)
        atp = _attn_prompt(qp, kbp, vbp, _kmean(kp), bp, sp)
        xp = _row_layers(xp, hp, pyp, atp, rest, l, tp)
        kp_l.append(kp.reshape(bp, sp // PAGE_SIZE, PAGE_SIZE, N_HEADS, HEAD_DIM))
        vp_l.append(vp.reshape(bp, sp // PAGE_SIZE, PAGE_SIZE, N_HEADS, HEAD_DIM))
        pp_l.append(up3[:, sp - POOL_STATE:, :])

        hs, us, qs, ks, vs, _, _ = _inproj(xs, gm, win, cos_s, sin_s, l, ts_rows)
        us3 = us.reshape(bs, ls, dp)
        prev_s = jnp.pad(state_pool[l], ((0, 0), (POOL_HALO - POOL_STATE, 0), (0, 0)))
        pys = _pool(us3, prev_s, wgrp, psc, l, ls, past).reshape(ts_rows, dp)
        sel = _gate(kmean_all, qs, l, bs, ls)
        sel_flat = sel[:, :MOBA_TOPK, :].transpose(0, 2, 1).reshape(-1)
        ats = _sattn(pt_flat, sel_flat, qs, ks, vs, cache_k, cache_v, l, bs, ls, n_pages).astype(BF16)
        xs = _row_layers(xs, hs, pys, ats, rest, l, ts_rows)
        ks_l.append(ks.reshape(bs, ls, N_HEADS, HEAD_DIM))
        vs_l.append(vs.reshape(bs, ls, N_HEADS, HEAD_DIM))
        ps_l.append(jnp.concatenate([state_pool[l], us3], axis=1)[:, -POOL_STATE:, :])

    gf = g_final.reshape(1, d)
    y_prompt = _final_norm(xp, gf, tp).reshape(bp, sp, d)
    y_sample = _final_norm(xs, gf, ts_rows).reshape(bs, ls, d)
    return (y_prompt, y_sample, jnp.stack(kp_l), jnp.stack(vp_l), jnp.stack(pp_l),
            jnp.stack(ks_l), jnp.stack(vs_l), jnp.stack(ps_l))
```

```python
import functools

import numpy as np
import jax
import jax.numpy as jnp
from jax import lax
from jax.experimental import pallas as pl
from jax.experimental.pallas import tpu as pltpu

F32 = jnp.float32
BF16 = jnp.bfloat16
I32 = jnp.int32
HIGHEST = lax.Precision.HIGHEST

N_HEADS = 8
HEAD_DIM = 128
POOL_WINDOWS = (2, 4, 8, 16)
POOL_GROUP = 256
POOL_STATE = 15
POOL_HALO = 16
MOBA_BLOCK = 256
MOBA_TOPK = 3
PAGE_SIZE = 128
PAGES_PER_BLOCK = MOBA_BLOCK // PAGE_SIZE
ROPE_THETA = 10000.0
EPS = 1e-6
NEG = -1e30
SM_SCALE = HEAD_DIM ** -0.5
LOG2E = 1.4426950408889634

VMEM_LIMIT = 52 * 1024 * 1024
ROW_TILE = 256
POOL_TILE = 512
NT_DIMS = (((1,), (1,)), ((), ()))


def _cparams(sem):
    return pltpu.CompilerParams(dimension_semantics=sem, vmem_limit_bytes=VMEM_LIMIT)


def _rms(x, g):
    ms = jnp.mean(x * x, axis=-1, keepdims=True)
    return x * lax.rsqrt(ms + EPS) * g


def _inproj_kernel(x_ref, g_ref, w_ref, cos_ref, sin_ref, h_ref, u_ref, q_ref, k_ref, v_ref, kb_ref, vb_ref, km_ref):
    tm, n = u_ref.shape
    h = _rms(x_ref[...], g_ref[...]).astype(BF16)
    h_ref[...] = h

    def rope(a):
        c = cos_ref[...]
        s = sin_ref[...]
        parts = []
        for hh in range(N_HEADS):
            ah = a[:, hh * HEAD_DIM:(hh + 1) * HEAD_DIM]
            parts.append(ah * c + pltpu.roll(ah, HEAD_DIM // 2, axis=1) * s)
        return jnp.concatenate(parts, axis=1)

    proj = lambda part: jnp.dot(h, w_ref[:, part * n:(part + 1) * n], preferred_element_type=F32)
    u_ref[...] = proj(0)
    q_ref[...] = rope(proj(1))
    k = rope(proj(2))
    k_ref[...] = k
    kb_ref[...] = k.astype(BF16)
    km_ref[0] = jnp.sum(k, axis=0, keepdims=True) * (1.0 / tm)
    v = proj(3)
    v_ref[...] = v
    vb_ref[...] = v.astype(BF16)


def _resident(block_shape, index_map):
    return pl.BlockSpec(block_shape, index_map, pipeline_mode=pl.Buffered(1))


def _inproj(x, g, w, cos, sin, layer, tm):
    t, d = x.shape
    n = w.shape[2] // 4
    pos_blocks = cos.shape[0] // tm
    row = lambda i: (i, 0)
    tab = lambda i: (i % pos_blocks, 0)
    return pl.pallas_call(
        _inproj_kernel,
        grid=(t // tm,),
        in_specs=[
            pl.BlockSpec((tm, d), row),
            _resident((None, 1, d), lambda i: (layer, 0, 0)),
            _resident((None, d, 4 * n), lambda i: (layer, 0, 0)),
            pl.BlockSpec((tm, HEAD_DIM), tab),
            pl.BlockSpec((tm, HEAD_DIM), tab),
        ],
        out_specs=[pl.BlockSpec((tm, d), row)] + [pl.BlockSpec((tm, n), row)] * 6
        + [pl.BlockSpec((1, 1, n), lambda i: (i, 0, 0))],
        out_shape=[jax.ShapeDtypeStruct((t, d), BF16)]
        + [jax.ShapeDtypeStruct((t, n), F32)] * 4
        + [jax.ShapeDtypeStruct((t, n), BF16)] * 2
        + [jax.ShapeDtypeStruct((t // tm, 1, n), F32)],
        compiler_params=_cparams(("arbitrary",)),
        name="inproj",
    )(x, g, w, cos, sin)


def _pool_kernel(*refs, ts, n_tiles, start):
    if n_tiles > 1:
        u_ref, halo_ref, prev_ref, wg_ref, sc_ref, y_ref, z = refs
    else:
        u_ref, prev_ref, wg_ref, sc_ref, y_ref, z = refs
    j = pl.program_id(1)
    z[POOL_HALO:POOL_HALO + ts, :] = u_ref[0]
    if n_tiles > 1:
        @pl.when(j == 0)
        def _():
            z[0:POOL_HALO, :] = prev_ref[0]

        @pl.when(j > 0)
        def _():
            z[0:POOL_HALO, :] = halo_ref[0]
    else:
        z[0:POOL_HALO, :] = prev_ref[0]

    pos = start + j * ts + lax.broadcasted_iota(I32, (ts, 1), 0)
    for g, w in enumerate(POOL_WINDOWS):
        cols = slice(g * POOL_GROUP, (g + 1) * POOL_GROUP)
        u = z[POOL_HALO:POOL_HALO + ts, cols]
        s = u
        for i in range(1, w):
            s = s + z[POOL_HALO - i:POOL_HALO - i + ts, cols]
        cnt = jnp.minimum(w, pos + 1).astype(F32)
        p = s / cnt - u
        y = jnp.dot(p.astype(BF16), wg_ref[g], preferred_element_type=F32) * sc_ref[:, cols]
        y_ref[0, :, cols] = y.astype(BF16)


def _pool(u, prev16, wg, scale, layer, ts, start):
    b, s, dp = u.shape
    n_tiles = s // ts
    in_specs = [pl.BlockSpec((1, ts, dp), lambda bi, j: (bi, j, 0))]
    args = [u]
    if n_tiles > 1:
        hb = ts // POOL_HALO
        in_specs.append(pl.BlockSpec((1, POOL_HALO, dp), lambda bi, j: (bi, jnp.maximum(j * hb - 1, 0), 0)))
        args.append(u)
    in_specs += [
        pl.BlockSpec((1, POOL_HALO, dp), lambda bi, j: (bi, 0, 0)),
        pl.BlockSpec((None,) + wg.shape[1:], lambda bi, j: (layer, 0, 0, 0)),
        pl.BlockSpec((None, 1, dp), lambda bi, j: (layer, 0, 0)),
    ]
    args += [prev16, wg, scale]
    return pl.pallas_call(
        functools.partial(_pool_kernel, ts=ts, n_tiles=n_tiles, start=start),
        grid=(b, n_tiles),
        in_specs=in_specs,
        out_specs=pl.BlockSpec((1, ts, dp), lambda bi, j: (bi, j, 0)),
        out_shape=jax.ShapeDtypeStruct((b, s, dp), BF16),
        scratch_shapes=[pltpu.VMEM((POOL_HALO + ts, dp), F32)],
        compiler_params=_cparams(("arbitrary", "arbitrary")),
        name="pool",
    )(*args)


def _topk_mask(g_t, valid):
    nb = g_t.shape[0]
    jidx = lax.broadcasted_iota(I32, g_t.shape, 0)
    g = jnp.where(valid, g_t, -jnp.inf)
    rank = jnp.zeros(g_t.shape, I32)
    for j in range(nb):
        row = g[j:j + 1, :]
        beats = (row > g) | ((row == g) & (jidx > j))
        rank = rank + beats.astype(I32)
    return valid & (rank < MOBA_TOPK), rank


ATTN_HEADS_PER_STEP = 2
ATTN_CLASS_BLOCKS = 4
BLOCKS_PER_GROUP = 8
PAGES_PER_GROUP = BLOCKS_PER_GROUP * PAGES_PER_BLOCK


def _group_block_means(page_row):
    sums = []
    for t in range(PAGES_PER_GROUP):
        parts = [page_row(t, rr) for rr in range(4)]
        for rr in range(4, PAGE_SIZE):
            parts[rr % 4] = parts[rr % 4] + page_row(t, rr)
        acc = (parts[0] + parts[1]) + (parts[2] + parts[3])
        blk_sum = acc if t % PAGES_PER_BLOCK == 0 else blk_sum + acc
        if t % PAGES_PER_BLOCK == PAGES_PER_BLOCK - 1:
            sums.append(blk_sum * (1.0 / MOBA_BLOCK))
    sub = lax.broadcasted_iota(I32, (BLOCKS_PER_GROUP, HEAD_DIM), 0)
    tiles = []
    for hh in range(N_HEADS):
        tile = jnp.zeros((BLOCKS_PER_GROUP, HEAD_DIM), F32)
        for tb in range(BLOCKS_PER_GROUP):
            row = jnp.broadcast_to(sums[tb][hh:hh + 1, :], (BLOCKS_PER_GROUP, HEAD_DIM))
            tile = jnp.where(sub == tb, row, tile)
        tiles.append(tile)
    return tiles


def _attn_prompt_kernel(pt_ref, q_ref, kb_ref, vb_ref, km_ref, ck_hbm, o_ref, ksum_ref,
                        vt, s_scr, p_scr, pbuf, psem, *, hg, nb, layer, units_per_group, n_units):
    i = pl.program_id(2)
    blk = MOBA_BLOCK
    hcols = lambda hh: slice(hh * HEAD_DIM, (hh + 1) * HEAD_DIM)

    @pl.when(i < units_per_group)
    def _():
        u = (pl.program_id(0) * pl.num_programs(1) + pl.program_id(1)) * units_per_group + i

        def copies(unit, slot):
            return [pltpu.make_async_copy(ck_hbm.at[layer, pt_ref[unit * PAGES_PER_GROUP + t]],
                                          pbuf.at[slot, t], psem.at[slot])
                    for t in range(PAGES_PER_GROUP)]

        @pl.when(u == 0)
        def _():
            for cp in copies(u, 0):
                cp.start()

        @pl.when(u + 1 < n_units)
        def _():
            for cp in copies(u + 1, (u + 1) & 1):
                cp.start()

        slot = u & 1
        for cp in copies(u, slot):
            cp.wait()
        tiles = _group_block_means(lambda t, r: pbuf[slot, t, r])
        for hh in range(N_HEADS):
            ksum_ref[0, hh] = tiles[hh]

    @pl.when(i == 0)
    def _():
        for hh in range(hg):
            for j in range(nb):
                vblk = vb_ref[j * blk:(j + 1) * blk, hcols(hh)]
                vt[hh, :, j * blk:(j + 1) * blk] = vblk.astype(F32).T.astype(BF16)

    rc = lax.broadcasted_iota(I32, (blk, blk), 0) - lax.broadcasted_iota(I32, (blk, blk), 1)
    jidx = lax.broadcasted_iota(I32, (nb, blk), 0)

    def attend(hh, n, bias, qs):
        m = None
        for j in range(n):
            s = jnp.dot(kb_ref[j * blk:(j + 1) * blk, hcols(hh)], qs, preferred_element_type=F32)
            s = s + bias[j:j + 1, :]
            if j >= n - ATTN_CLASS_BLOCKS:
                s = jnp.where(rc <= jnp.where(i == j, 0, blk), s, NEG)
            s_scr[hh, j * blk:(j + 1) * blk, :] = s
            cm = jnp.max(s, axis=0, keepdims=True)
            m = cm if m is None else jnp.maximum(m, cm)
        l = jnp.zeros_like(m)
        for j in range(n):
            p = jnp.exp2(s_scr[hh, j * blk:(j + 1) * blk, :] - m)
            l = l + jnp.sum(p, axis=0, keepdims=True)
            p_scr[hh, j * blk:(j + 1) * blk, :] = p.astype(BF16)
        acc = jnp.dot(vt[hh, :, 0:n * blk], p_scr[hh, 0:n * blk, :], preferred_element_type=F32)
        o_ref[:, hcols(hh)] = (acc / l).T.astype(BF16)

    biases, qss = [], []
    for hh in range(hg):
        q_t = q_ref[:, hcols(hh)].T
        g_t = jnp.dot(km_ref[:, hcols(hh)], q_t, precision=HIGHEST, preferred_element_type=F32)
        sel, _ = _topk_mask(g_t, jidx < i)
        biases.append(jnp.where(sel | (jidx == i), 0.0, NEG))
        qss.append((q_t * (SM_SCALE * LOG2E)).astype(BF16))
    for n in range(ATTN_CLASS_BLOCKS, nb + 1, ATTN_CLASS_BLOCKS):
        @pl.when((i >= n - ATTN_CLASS_BLOCKS) & (i < n))
        def _(n=n):
            for hh in range(hg):
                attend(hh, n, biases[hh], qss[hh])


def _attn_prompt(pt_flat, q, kb, vb, kmean, cache_k, layer, batch, seq, n_seq, n_pages):
    t, da = q.shape
    nb = seq // MOBA_BLOCK
    hg = ATTN_HEADS_PER_STEP
    n_groups = N_HEADS // hg
    wcols = hg * HEAD_DIM
    units_per_seq = n_pages // PAGES_PER_GROUP
    n_units = n_seq * units_per_seq
    units_per_group = n_units // (batch * n_groups)
    assert units_per_group * batch * n_groups == n_units and units_per_group <= nb
    qmap = lambda b, g, i, pt: (b * nb + i, g)
    kvmap = lambda b, g, i, pt: (b, g)

    def ksum_map(b, g, i, pt):
        u = (b * n_groups + g) * units_per_group + jnp.minimum(i, units_per_group - 1)
        return (u // units_per_seq, 0, u % units_per_seq, 0)

    return pl.pallas_call(
        functools.partial(_attn_prompt_kernel, hg=hg, nb=nb, layer=layer,
                          units_per_group=units_per_group, n_units=n_units),
        grid_spec=pltpu.PrefetchScalarGridSpec(
            num_scalar_prefetch=1,
            grid=(batch, n_groups, nb),
            in_specs=[
                pl.BlockSpec((MOBA_BLOCK, wcols), qmap),
                pl.BlockSpec((seq, wcols), kvmap),
                pl.BlockSpec((seq, wcols), kvmap),
                pl.BlockSpec((nb, wcols), kvmap),
                pl.BlockSpec(memory_space=pl.ANY),
            ],
            out_specs=[
                pl.BlockSpec((MOBA_BLOCK, wcols), qmap),
                pl.BlockSpec((1, N_HEADS, BLOCKS_PER_GROUP, HEAD_DIM), ksum_map),
            ],
            scratch_shapes=[
                pltpu.VMEM((hg, HEAD_DIM, seq), BF16),
                pltpu.VMEM((hg, seq, MOBA_BLOCK), F32),
                pltpu.VMEM((hg, seq, MOBA_BLOCK), BF16),
                pltpu.VMEM((2, PAGES_PER_GROUP, PAGE_SIZE, N_HEADS, HEAD_DIM), F32),
                pltpu.SemaphoreType.DMA((2,)),
            ],
        ),
        out_shape=[
            jax.ShapeDtypeStruct((t, da), BF16),
            jax.ShapeDtypeStruct((n_seq, N_HEADS, n_pages // PAGES_PER_BLOCK, HEAD_DIM), F32),
        ],
        compiler_params=_cparams(("arbitrary", "arbitrary", "arbitrary")),
        name="attn_prompt",
    )(pt_flat, q, kb, vb, kmean, cache_k)


def _gate_kernel(km_ref, q_ref, sel_ref):
    lq = q_ref.shape[0]
    nblk = km_ref.shape[2]
    ncol = N_HEADS * lq
    rows = lax.broadcasted_iota(I32, (ncol, HEAD_DIM), 0)
    g_t = jnp.zeros((nblk, ncol), F32)
    for hh in range(N_HEADS):
        qh = q_ref[:, hh * HEAD_DIM:(hh + 1) * HEAD_DIM]
        qpad = jnp.concatenate([qh] * N_HEADS, axis=0)
        qpad = jnp.where((rows >= hh * lq) & (rows < (hh + 1) * lq), qpad, 0.0)
        g_t = g_t + lax.dot_general(km_ref[0, hh], qpad, NT_DIMS, precision=HIGHEST,
                                    preferred_element_type=F32)
    valid = jnp.full(g_t.shape, True)
    _, rank = _topk_mask(g_t, valid)
    bidx = lax.broadcasted_iota(I32, g_t.shape, 0).astype(F32)
    out = []
    for r in range(MOBA_TOPK):
        out.append(jnp.sum(jnp.where(rank == r, bidx, 0.0), axis=0, keepdims=True))
    out.append(jnp.zeros((8 - MOBA_TOPK, ncol), F32))
    sel_ref[0] = jnp.concatenate(out, axis=0).astype(I32)


def _gate(kmean, q, n_seq, lq):
    nblk = kmean.shape[2]
    ncol = N_HEADS * lq
    return pl.pallas_call(
        _gate_kernel,
        grid=(n_seq,),
        in_specs=[
            pl.BlockSpec((1, N_HEADS, nblk, HEAD_DIM), lambda b: (b, 0, 0, 0)),
            pl.BlockSpec((lq, N_HEADS * HEAD_DIM), lambda b: (b, 0)),
        ],
        out_specs=pl.BlockSpec((1, 8, ncol), lambda b: (b, 0, 0)),
        out_shape=jax.ShapeDtypeStruct((n_seq, 8, ncol), I32),
        compiler_params=_cparams(("arbitrary",)),
        name="gate",
    )(kmean, q)


def _sattn_kernel(pt_ref, sel_ref, q_ref, kn_ref, vn_ref, ck_hbm, cv_hbm, o_ref, kbuf, vbuf, sem,
                  *, layer, n_pages, lq):
    s = pl.program_id(0)
    nsteps = pl.num_programs(0)
    nseg = lq * MOBA_TOPK

    def copies(step, slot):
        b = lax.shift_right_logical(step, 3)
        hh = step & (N_HEADS - 1)
        out = []
        for qq in range(lq):
            lanes = pl.ds(qq * HEAD_DIM, HEAD_DIM)
            for r in range(MOBA_TOPK):
                blk = sel_ref[step * nseg + qq * MOBA_TOPK + r]
                for half in range(PAGES_PER_BLOCK):
                    page = pt_ref[b * n_pages + blk * PAGES_PER_BLOCK + half]
                    rows = pl.ds(r * MOBA_BLOCK + half * PAGE_SIZE, PAGE_SIZE)
                    out.append(pltpu.make_async_copy(ck_hbm.at[layer, page, :, hh, :], kbuf.at[slot, rows, lanes], sem.at[0, slot]))
                    out.append(pltpu.make_async_copy(cv_hbm.at[layer, page, :, hh, :], vbuf.at[slot, rows, lanes], sem.at[1, slot]))
        return out

    @pl.when(s == 0)
    def _():
        for cp in copies(s, 0):
            cp.start()

    @pl.when(s + 1 < nsteps)
    def _():
        for cp in copies(s + 1, (s + 1) & 1):
            cp.start()

    slot = s & 1
    for cp in copies(s, slot):
        cp.wait()

    qb = (q_ref[...] * SM_SCALE).astype(BF16)
    q_bd = jnp.concatenate([qb] * lq, axis=1)
    rq = lax.broadcasted_iota(I32, q_bd.shape, 0)
    cq = lax.broadcasted_iota(I32, q_bd.shape, 1)
    own_lanes = (cq >= rq * HEAD_DIM) & (cq < (rq + 1) * HEAD_DIM)
    q_bd = jnp.where(own_lanes, q_bd, jnp.zeros_like(q_bd))
    s_t = lax.dot_general(kbuf[slot].astype(BF16), q_bd, NT_DIMS, preferred_element_type=F32)
    sn_t = lax.dot_general(kn_ref[...].astype(BF16), qb, NT_DIMS, preferred_element_type=F32)
    rn = lax.broadcasted_iota(I32, sn_t.shape, 0)
    cn = lax.broadcasted_iota(I32, sn_t.shape, 1)
    sn_t = jnp.where(rn <= cn, sn_t, NEG)
    m = jnp.maximum(jnp.max(s_t, axis=0, keepdims=True), jnp.max(sn_t, axis=0, keepdims=True))
    p = jnp.exp(s_t - m)
    pn = jnp.exp(sn_t - m)
    inv_l = 1.0 / (jnp.sum(p, axis=0, keepdims=True) + jnp.sum(pn, axis=0, keepdims=True))
    p = p * inv_l
    pn = pn * inv_l
    vn = vn_ref[...]
    rows = []
    for qq in range(lq):
        lanes = slice(qq * HEAD_DIM, (qq + 1) * HEAD_DIM)
        o = jnp.sum(p[:, qq:qq + 1] * vbuf[slot, :, lanes], axis=0, keepdims=True)
        o = o + jnp.sum(pn[:, qq:qq + 1] * vn, axis=0, keepdims=True)
        rows.append(o)
    o_ref[...] = jnp.concatenate(rows, axis=0)


def _sattn(pt_flat, sel_flat, q, kn, vn, cache_k, cache_v, layer, n_seq, lq, n_pages):
    nseg = lq * MOBA_TOPK
    blk = lambda s, pt, sel: (lax.shift_right_logical(s, 3), s & (N_HEADS - 1))
    spec = pl.BlockSpec((lq, HEAD_DIM), blk)
    return pl.pallas_call(
        functools.partial(_sattn_kernel, layer=layer, n_pages=n_pages, lq=lq),
        grid_spec=pltpu.PrefetchScalarGridSpec(
            num_scalar_prefetch=2,
            grid=(n_seq * N_HEADS,),
            in_specs=[spec, spec, spec, pl.BlockSpec(memory_space=pl.ANY), pl.BlockSpec(memory_space=pl.ANY)],
            out_specs=spec,
            scratch_shapes=[
                pltpu.VMEM((2, MOBA_TOPK * MOBA_BLOCK, lq * HEAD_DIM), F32),
                pltpu.VMEM((2, MOBA_TOPK * MOBA_BLOCK, lq * HEAD_DIM), F32),
                pltpu.SemaphoreType.DMA((2, 2)),
            ],
        ),
        out_shape=jax.ShapeDtypeStruct(q.shape, F32),
        compiler_params=_cparams(("arbitrary",)),
        name="sattn",
    )(pt_flat, sel_flat, q, kn, vn, cache_k, cache_v)


COL_CHUNK = 512


def _merge_kernel(h_ref, wg_ref, py_ref, wpp_ref, at_ref, wap_ref, o_ref):
    d = o_ref.shape[1]
    h = h_ref[...]
    py = py_ref[...]
    at = at_ref[...]
    for c0 in range(0, d, COL_CHUNK):
        cols = slice(c0, c0 + COL_CHUNK)
        g1 = jax.nn.sigmoid(jnp.dot(h, wg_ref[:, cols], preferred_element_type=F32))
        g2 = jax.nn.sigmoid(jnp.dot(h, wg_ref[:, d + c0:d + c0 + COL_CHUNK], preferred_element_type=F32))
        a = jnp.dot(py, wpp_ref[:, cols], preferred_element_type=F32)
        b = jnp.dot(at, wap_ref[:, cols], preferred_element_type=F32)
        o_ref[:, cols] = (g1 * a + g2 * b).astype(BF16)


def _merge(h, wgate, py, wpp, at, wap, layer, tm):
    t, d = h.shape
    dp = py.shape[1]
    da = at.shape[1]
    row = lambda i: (i, 0)
    lay = lambda i: (layer, 0, 0)
    return pl.pallas_call(
        _merge_kernel,
        grid=(t // tm,),
        in_specs=[
            pl.BlockSpec((tm, d), row),
            _resident((None, d, 2 * d), lay),
            pl.BlockSpec((tm, dp), row),
            _resident((None, dp, d), lay),
            pl.BlockSpec((tm, da), row),
            _resident((None, da, d), lay),
        ],
        out_specs=pl.BlockSpec((tm, d), row),
        out_shape=jax.ShapeDtypeStruct((t, d), BF16),
        compiler_params=_cparams(("arbitrary",)),
        name="merge",
    )(h, wgate, py, wpp, at, wap)


def _outproj_kernel(a_ref, w_ref, x_ref, g_ref, x1_ref, h2_ref):
    d = x1_ref.shape[1]
    a = a_ref[...]
    for c0 in range(0, d, COL_CHUNK):
        cols = slice(c0, c0 + COL_CHUNK)
        x1_ref[:, cols] = x_ref[:, cols] + jnp.dot(a, w_ref[:, cols], preferred_element_type=F32)
    h2_ref[...] = _rms(x1_ref[...], g_ref[...]).astype(BF16)


def _outproj(a, w, x, g, layer, tm):
    t, kd = a.shape
    d = w.shape[2]
    row = lambda i: (i, 0)
    return pl.pallas_call(
        _outproj_kernel,
        grid=(t // tm,),
        in_specs=[
            pl.BlockSpec((tm, kd), row),
            _resident((None, kd, d), lambda i: (layer, 0, 0)),
            pl.BlockSpec((tm, d), row),
            _resident((None, 1, d), lambda i: (layer, 0, 0)),
        ],
        out_specs=[pl.BlockSpec((tm, d), row), pl.BlockSpec((tm, d), row)],
        out_shape=[jax.ShapeDtypeStruct((t, d), F32), jax.ShapeDtypeStruct((t, d), BF16)],
        compiler_params=_cparams(("arbitrary",)),
        name="outproj",
    )(a, w, x, g)


def _mlp_up_kernel(h2_ref, w_ref, o_ref):
    n = o_ref.shape[1]
    h2 = h2_ref[...]
    for c0 in range(0, n, COL_CHUNK):
        cols = slice(c0, c0 + COL_CHUNK)
        acc = jnp.dot(h2, w_ref[:, cols], preferred_element_type=F32)
        o_ref[:, cols] = jnp.square(jnp.maximum(acc, 0.0)).astype(BF16)


def _mlp_up(h2, w, layer, tm):
    t, d = h2.shape
    n = w.shape[2]
    row = lambda i: (i, 0)
    return pl.pallas_call(
        _mlp_up_kernel,
        grid=(t // tm,),
        in_specs=[pl.BlockSpec((tm, d), row), _resident((None, d, n), lambda i: (layer, 0, 0))],
        out_specs=pl.BlockSpec((tm, n), row),
        out_shape=jax.ShapeDtypeStruct((t, n), BF16),
        compiler_params=_cparams(("arbitrary",)),
        name="mlp_up",
    )(h2, w)


def _mlp_down_kernel(a_ref, w_ref, x_ref, o_ref):
    d = o_ref.shape[1]
    a = a_ref[...]
    for c0 in range(0, d, COL_CHUNK):
        cols = slice(c0, c0 + COL_CHUNK)
        o_ref[:, cols] = x_ref[:, cols] + jnp.dot(a, w_ref[:, cols], preferred_element_type=F32)


def _mlp_down(a, w, x, layer, tm):
    t, n = a.shape
    d = w.shape[2]
    row = lambda i: (i, 0)
    return pl.pallas_call(
        _mlp_down_kernel,
        grid=(t // tm,),
        in_specs=[
            pl.BlockSpec((tm, n), row),
            _resident((None, n, d), lambda i: (layer, 0, 0)),
            pl.BlockSpec((tm, d), row),
        ],
        out_specs=pl.BlockSpec((tm, d), row),
        out_shape=jax.ShapeDtypeStruct((t, d), F32),
        compiler_params=_cparams(("arbitrary",)),
        name="mlp_down",
    )(a, w, x)


def _final_norm_kernel(x_ref, g_ref, o_ref):
    o_ref[...] = _rms(x_ref[...], g_ref[...])


def _final_norm(x, g, tm):
    t, d = x.shape
    return pl.pallas_call(
        _final_norm_kernel,
        grid=(t // tm,),
        in_specs=[pl.BlockSpec((tm, d), lambda i: (i, 0)), pl.BlockSpec((1, d), lambda i: (0, 0))],
        out_specs=pl.BlockSpec((tm, d), lambda i: (i, 0)),
        out_shape=jax.ShapeDtypeStruct((t, d), F32),
        compiler_params=_cparams(("arbitrary",)),
        name="final_norm",
    )(x, g)


def _rope_tables(pos):
    inv = ROPE_THETA ** (-np.arange(0, HEAD_DIM, 2, dtype=np.float64) / HEAD_DIM)
    ang = np.asarray(pos, np.float64)[:, None] * inv[None, :]
    cos = np.concatenate([np.cos(ang), np.cos(ang)], axis=1)
    sin = np.concatenate([-np.sin(ang), np.sin(ang)], axis=1)
    return jnp.asarray(cos, F32), jnp.asarray(sin, F32)


def _row_layers(x, h, py, at, weights, layer, tm):
    wgate, wpp, wap, wout, gmlp, wup, wdown = weights
    merged = _merge(h, wgate, py, wpp, at, wap, layer, tm)
    x1, h2 = _outproj(merged, wout, x, gmlp, layer, tm)
    a = _mlp_up(h2, wup, layer, tm)
    return _mlp_down(a, wdown, x1, layer, tm)


def kernel(x_prompt, x_sample, cache_k, cache_v, state_pool, page_table, g_mix, w_in, w_pool_grp, pool_scale, w_pool_proj, w_att_proj, w_gate, w_out, g_mlp, w_up, w_down, g_final):
    bp, sp, d = x_prompt.shape
    bs, ls, _ = x_sample.shape
    depth = w_in.shape[0]
    n_pages = page_table.shape[1]
    past = n_pages * PAGE_SIZE
    dp = pool_scale.shape[1]
    tp = ROW_TILE
    assert tp == MOBA_BLOCK
    ts_rows = bs * ls

    xp = x_prompt.reshape(bp * sp, d)
    xs = x_sample.reshape(ts_rows, d)
    cos_p, sin_p = _rope_tables(np.arange(sp))
    cos_s, sin_s = _rope_tables(np.tile(past + np.arange(ls), bs))
    pt_flat = page_table.reshape(-1)
    prev_p = jnp.zeros((bp, POOL_HALO, dp), F32)

    gm = g_mix.reshape(depth, 1, d)
    win = w_in.astype(BF16)
    wgrp = w_pool_grp.astype(BF16)
    psc = pool_scale.reshape(depth, 1, dp)
    rest = (w_gate.astype(BF16), w_pool_proj.astype(BF16), w_att_proj.astype(BF16),
            w_out.astype(BF16), g_mlp.reshape(depth, 1, d), w_up.astype(BF16), w_down.astype(BF16))

    kp_l, vp_l, pp_l, ks_l, vs_l, ps_l = [], [], [], [], [], []
    for l in range(depth):
        hp, up, qp, kp, vp, kbp, vbp, kmp = _inproj(xp, gm, win, cos_p, sin_p, l, tp)
        up3 = up.reshape(bp, sp, dp)
        pyp = _pool(up3, prev_p, wgrp, psc, l, POOL_TILE, 0).reshape(bp * sp, dp)
        atp, kmean_cache = _attn_prompt(pt_flat, qp, kbp, vbp, kmp.reshape(bp * sp // MOBA_BLOCK, -1),
                                        cache_k, l, bp, sp, bs, n_pages)
        xp = _row_layers(xp, hp, pyp, atp, rest, l, tp)
        kp_l.append(kp.reshape(bp, sp // PAGE_SIZE, PAGE_SIZE, N_HEADS, HEAD_DIM))
        vp_l.append(vp.reshape(bp, sp // PAGE_SIZE, PAGE_SIZE, N_HEADS, HEAD_DIM))
        pp_l.append(up3[:, sp - POOL_STATE:, :])

        hs, us, qs, ks, vs, _, _, _ = _inproj(xs, gm, win, cos_s, sin_s, l, ts_rows)
        us3 = us.reshape(bs, ls, dp)
        prev_s = jnp.pad(state_pool[l], ((0, 0), (POOL_HALO - POOL_STATE, 0), (0, 0)))
        pys = _pool(us3, prev_s, wgrp, psc, l, ls, past).reshape(ts_rows, dp)
        sel = _gate(kmean_cache, qs, bs, ls)
        sel_flat = sel[:, :MOBA_TOPK, :].transpose(0, 2, 1).reshape(-1)
        ats = _sattn(pt_flat, sel_flat, qs, ks, vs, cache_k, cache_v, l, bs, ls, n_pages).astype(BF16)
        xs = _row_layers(xs, hs, pys, ats, rest, l, ts_rows)
        ks_l.append(ks.reshape(bs, ls, N_HEADS, HEAD_DIM))
        vs_l.append(vs.reshape(bs, ls, N_HEADS, HEAD_DIM))
        ps_l.append(jnp.concatenate([state_pool[l], us3], axis=1)[:, -POOL_STATE:, :])

    gf = g_final.reshape(1, d)
    y_prompt = _final_norm(xp, gf, tp).reshape(bp, sp, d)
    y_sample = _final_norm(xs, gf, ts_rows).reshape(bs, ls, d)
    return (y_prompt, y_sample, jnp.stack(kp_l), jnp.stack(vp_l), jnp.stack(pp_l),
            jnp.stack(ks_l), jnp.stack(vs_l), jnp.stack(ps_l))
```

```python
import functools

import numpy as np
import jax
import jax.numpy as jnp
from jax import lax
from jax.experimental import pallas as pl
from jax.experimental.pallas import tpu as pltpu

F32 = jnp.float32
BF16 = jnp.bfloat16
I32 = jnp.int32
HIGHEST = lax.Precision.HIGHEST

N_HEADS = 8
HEAD_DIM = 128
POOL_WINDOWS = (2, 4, 8, 16)
POOL_GROUP = 256
POOL_STATE = 15
POOL_HALO = 16
MOBA_BLOCK = 256
MOBA_TOPK = 3
PAGE_SIZE = 128
PAGES_PER_BLOCK = MOBA_BLOCK // PAGE_SIZE
ROPE_THETA = 10000.0
EPS = 1e-6
NEG = -1e30
SM_SCALE = HEAD_DIM ** -0.5
LOG2E = 1.4426950408889634

VMEM_LIMIT = 52 * 1024 * 1024
ROW_TILE = 256
POOL_TILE = 512
NT_DIMS = (((1,), (1,)), ((), ()))


def _cparams(sem):
    return pltpu.CompilerParams(dimension_semantics=sem, vmem_limit_bytes=VMEM_LIMIT)


def _rms(x, g):
    ms = jnp.mean(x * x, axis=-1, keepdims=True)
    return x * lax.rsqrt(ms + EPS) * g


def _inproj_kernel(x_ref, g_ref, w_ref, cos_ref, sin_ref, h_ref, u_ref, q_ref, k_ref, v_ref, kb_ref, vb_ref, km_ref):
    tm, n = u_ref.shape
    h = _rms(x_ref[...], g_ref[...]).astype(BF16)
    h_ref[...] = h

    def rope(a):
        c = cos_ref[...]
        s = sin_ref[...]
        parts = []
        for hh in range(N_HEADS):
            ah = a[:, hh * HEAD_DIM:(hh + 1) * HEAD_DIM]
            parts.append(ah * c + pltpu.roll(ah, HEAD_DIM // 2, axis=1) * s)
        return jnp.concatenate(parts, axis=1)

    proj = lambda part: jnp.dot(h, w_ref[:, part * n:(part + 1) * n], preferred_element_type=F32)
    u_ref[...] = proj(0)
    q_ref[...] = rope(proj(1))
    k = rope(proj(2))
    k_ref[...] = k
    kb_ref[...] = k.astype(BF16)
    km_ref[0] = jnp.sum(k, axis=0, keepdims=True) * (1.0 / tm)
    v = proj(3)
    v_ref[...] = v
    vb_ref[...] = v.astype(BF16)


def _resident(block_shape, index_map):
    return pl.BlockSpec(block_shape, index_map, pipeline_mode=pl.Buffered(1))


def _inproj(x, g, w, cos, sin, layer, tm):
    t, d = x.shape
    n = w.shape[2] // 4
    pos_blocks = cos.shape[0] // tm
    row = lambda i: (i, 0)
    tab = lambda i: (i % pos_blocks, 0)
    return pl.pallas_call(
        _inproj_kernel,
        grid=(t // tm,),
        in_specs=[
            pl.BlockSpec((tm, d), row),
            _resident((None, 1, d), lambda i: (layer, 0, 0)),
            _resident((None, d, 4 * n), lambda i: (layer, 0, 0)),
            pl.BlockSpec((tm, HEAD_DIM), tab),
            pl.BlockSpec((tm, HEAD_DIM), tab),
        ],
        out_specs=[pl.BlockSpec((tm, d), row)] + [pl.BlockSpec((tm, n), row)] * 6
        + [pl.BlockSpec((1, 1, n), lambda i: (i, 0, 0))],
        out_shape=[jax.ShapeDtypeStruct((t, d), BF16)]
        + [jax.ShapeDtypeStruct((t, n), F32)] * 4
        + [jax.ShapeDtypeStruct((t, n), BF16)] * 2
        + [jax.ShapeDtypeStruct((t // tm, 1, n), F32)],
        compiler_params=_cparams(("arbitrary",)),
        name="inproj",
    )(x, g, w, cos, sin)


def _pool_kernel(*refs, ts, n_tiles, start):
    if n_tiles > 1:
        u_ref, halo_ref, prev_ref, wg_ref, sc_ref, y_ref, z = refs
    else:
        u_ref, prev_ref, wg_ref, sc_ref, y_ref, z = refs
    j = pl.program_id(1)
    z[POOL_HALO:POOL_HALO + ts, :] = u_ref[0]
    if n_tiles > 1:
        @pl.when(j == 0)
        def _():
            z[0:POOL_HALO, :] = prev_ref[0]

        @pl.when(j > 0)
        def _():
            z[0:POOL_HALO, :] = halo_ref[0]
    else:
        z[0:POOL_HALO, :] = prev_ref[0]

    pos = start + j * ts + lax.broadcasted_iota(I32, (ts, 1), 0)
    for g, w in enumerate(POOL_WINDOWS):
        cols = slice(g * POOL_GROUP, (g + 1) * POOL_GROUP)
        u = z[POOL_HALO:POOL_HALO + ts, cols]
        s = u
        for i in range(1, w):
            s = s + z[POOL_HALO - i:POOL_HALO - i + ts, cols]
        cnt = jnp.minimum(w, pos + 1).astype(F32)
        p = s / cnt - u
        y = jnp.dot(p.astype(BF16), wg_ref[g], preferred_element_type=F32) * sc_ref[:, cols]
        y_ref[0, :, cols] = y.astype(BF16)


def _pool(u, prev16, wg, scale, layer, ts, start):
    b, s, dp = u.shape
    n_tiles = s // ts
    in_specs = [pl.BlockSpec((1, ts, dp), lambda bi, j: (bi, j, 0))]
    args = [u]
    if n_tiles > 1:
        hb = ts // POOL_HALO
        in_specs.append(pl.BlockSpec((1, POOL_HALO, dp), lambda bi, j: (bi, jnp.maximum(j * hb - 1, 0), 0)))
        args.append(u)
    in_specs += [
        pl.BlockSpec((1, POOL_HALO, dp), lambda bi, j: (bi, 0, 0)),
        pl.BlockSpec((None,) + wg.shape[1:], lambda bi, j: (layer, 0, 0, 0)),
        pl.BlockSpec((None, 1, dp), lambda bi, j: (layer, 0, 0)),
    ]
    args += [prev16, wg, scale]
    return pl.pallas_call(
        functools.partial(_pool_kernel, ts=ts, n_tiles=n_tiles, start=start),
        grid=(b, n_tiles),
        in_specs=in_specs,
        out_specs=pl.BlockSpec((1, ts, dp), lambda bi, j: (bi, j, 0)),
        out_shape=jax.ShapeDtypeStruct((b, s, dp), BF16),
        scratch_shapes=[pltpu.VMEM((POOL_HALO + ts, dp), F32)],
        compiler_params=_cparams(("arbitrary", "arbitrary")),
        name="pool",
    )(*args)


def _topk_mask(g_t, valid):
    nb = g_t.shape[0]
    jidx = lax.broadcasted_iota(I32, g_t.shape, 0)
    g = jnp.where(valid, g_t, -jnp.inf)
    rank = jnp.zeros(g_t.shape, I32)
    for j in range(nb):
        row = g[j:j + 1, :]
        beats = (row > g) | ((row == g) & (jidx > j))
        rank = rank + beats.astype(I32)
    return valid & (rank < MOBA_TOPK), rank


ATTN_HEADS_PER_STEP = 2
ATTN_CLASS_BLOCKS = 2
BLOCKS_PER_GROUP = 8
PAGES_PER_GROUP = BLOCKS_PER_GROUP * PAGES_PER_BLOCK


def _group_block_means(page_row):
    sums = []
    for t in range(PAGES_PER_GROUP):
        parts = [page_row(t, rr) for rr in range(4)]
        for rr in range(4, PAGE_SIZE):
            parts[rr % 4] = parts[rr % 4] + page_row(t, rr)
        acc = (parts[0] + parts[1]) + (parts[2] + parts[3])
        blk_sum = acc if t % PAGES_PER_BLOCK == 0 else blk_sum + acc
        if t % PAGES_PER_BLOCK == PAGES_PER_BLOCK - 1:
            sums.append(blk_sum * (1.0 / MOBA_BLOCK))
    sub = lax.broadcasted_iota(I32, (BLOCKS_PER_GROUP, HEAD_DIM), 0)
    tiles = []
    for hh in range(N_HEADS):
        tile = jnp.zeros((BLOCKS_PER_GROUP, HEAD_DIM), F32)
        for tb in range(BLOCKS_PER_GROUP):
            row = jnp.broadcast_to(sums[tb][hh:hh + 1, :], (BLOCKS_PER_GROUP, HEAD_DIM))
            tile = jnp.where(sub == tb, row, tile)
        tiles.append(tile)
    return tiles


def _attn_prompt_kernel(pt_ref, q_ref, kb_ref, vb_ref, km_ref, ck_hbm, o_ref, ksum_ref,
                        vt, s_scr, p_scr, pbuf, psem, *, hg, nb, layer, units_per_group, n_units):
    i = pl.program_id(2)
    blk = MOBA_BLOCK
    hcols = lambda hh: slice(hh * HEAD_DIM, (hh + 1) * HEAD_DIM)

    @pl.when(i < units_per_group)
    def _():
        u = (pl.program_id(0) * pl.num_programs(1) + pl.program_id(1)) * units_per_group + i

        def copies(unit, slot):
            return [pltpu.make_async_copy(ck_hbm.at[layer, pt_ref[unit * PAGES_PER_GROUP + t]],
                                          pbuf.at[slot, t], psem.at[slot])
                    for t in range(PAGES_PER_GROUP)]

        @pl.when(u == 0)
        def _():
            for cp in copies(u, 0):
                cp.start()

        @pl.when(u + 1 < n_units)
        def _():
            for cp in copies(u + 1, (u + 1) & 1):
                cp.start()

        slot = u & 1
        for cp in copies(u, slot):
            cp.wait()
        tiles = _group_block_means(lambda t, r: pbuf[slot, t, r])
        for hh in range(N_HEADS):
            ksum_ref[0, hh] = tiles[hh]

    @pl.when(i == 0)
    def _():
        for hh in range(hg):
            for j in range(nb):
                vblk = vb_ref[j * blk:(j + 1) * blk, hcols(hh)]
                vt[hh, :, j * blk:(j + 1) * blk] = vblk.astype(F32).T.astype(BF16)

    rc = lax.broadcasted_iota(I32, (blk, blk), 0) - lax.broadcasted_iota(I32, (blk, blk), 1)
    jidx = lax.broadcasted_iota(I32, (nb, blk), 0)

    def attend(hh, n, bias, qs):
        m = None
        for j in range(n):
            s = jnp.dot(kb_ref[j * blk:(j + 1) * blk, hcols(hh)], qs, preferred_element_type=F32)
            s = s + bias[j:j + 1, :]
            if j >= n - ATTN_CLASS_BLOCKS:
                s = jnp.where(rc <= jnp.where(i == j, 0, blk), s, NEG)
            s_scr[hh, j * blk:(j + 1) * blk, :] = s
            cm = jnp.max(s, axis=0, keepdims=True)
            m = cm if m is None else jnp.maximum(m, cm)
        l = jnp.zeros_like(m)
        for j in range(n):
            p = jnp.exp2(s_scr[hh, j * blk:(j + 1) * blk, :] - m)
            l = l + jnp.sum(p, axis=0, keepdims=True)
            p_scr[hh, j * blk:(j + 1) * blk, :] = p.astype(BF16)
        acc = jnp.dot(vt[hh, :, 0:n * blk], p_scr[hh, 0:n * blk, :], preferred_element_type=F32)
        o_ref[:, hcols(hh)] = (acc / l).T.astype(BF16)

    biases, qss = [], []
    for hh in range(hg):
        q_t = q_ref[:, hcols(hh)].T
        g_t = jnp.dot(km_ref[:, hcols(hh)], q_t, precision=HIGHEST, preferred_element_type=F32)
        sel, _ = _topk_mask(g_t, jidx < i)
        biases.append(jnp.where(sel | (jidx == i), 0.0, NEG))
        qss.append((q_t * (SM_SCALE * LOG2E)).astype(BF16))
    for n in range(ATTN_CLASS_BLOCKS, nb + 1, ATTN_CLASS_BLOCKS):
        @pl.when((i >= n - ATTN_CLASS_BLOCKS) & (i < n))
        def _(n=n):
            for hh in range(hg):
                attend(hh, n, biases[hh], qss[hh])


def _attn_prompt(pt_flat, q, kb, vb, kmean, cache_k, layer, batch, seq, n_seq, n_pages):
    t, da = q.shape
    nb = seq // MOBA_BLOCK
    hg = ATTN_HEADS_PER_STEP
    n_groups = N_HEADS // hg
    wcols = hg * HEAD_DIM
    units_per_seq = n_pages // PAGES_PER_GROUP
    n_units = n_seq * units_per_seq
    units_per_group = n_units // (batch * n_groups)
    assert units_per_group * batch * n_groups == n_units and units_per_group <= nb
    qmap = lambda b, g, i, pt: (b * nb + i, g)
    kvmap = lambda b, g, i, pt: (b, g)

    def ksum_map(b, g, i, pt):
        u = (b * n_groups + g) * units_per_group + jnp.minimum(i, units_per_group - 1)
        return (u // units_per_seq, 0, u % units_per_seq, 0)

    return pl.pallas_call(
        functools.partial(_attn_prompt_kernel, hg=hg, nb=nb, layer=layer,
                          units_per_group=units_per_group, n_units=n_units),
        grid_spec=pltpu.PrefetchScalarGridSpec(
            num_scalar_prefetch=1,
            grid=(batch, n_groups, nb),
            in_specs=[
                pl.BlockSpec((MOBA_BLOCK, wcols), qmap),
                pl.BlockSpec((seq, wcols), kvmap),
                pl.BlockSpec((seq, wcols), kvmap),
                pl.BlockSpec((nb, wcols), kvmap),
                pl.BlockSpec(memory_space=pl.ANY),
            ],
            out_specs=[
                pl.BlockSpec((MOBA_BLOCK, wcols), qmap),
                pl.BlockSpec((1, N_HEADS, BLOCKS_PER_GROUP, HEAD_DIM), ksum_map),
            ],
            scratch_shapes=[
                pltpu.VMEM((hg, HEAD_DIM, seq), BF16),
                pltpu.VMEM((hg, seq, MOBA_BLOCK), F32),
                pltpu.VMEM((hg, seq, MOBA_BLOCK), BF16),
                pltpu.VMEM((2, PAGES_PER_GROUP, PAGE_SIZE, N_HEADS, HEAD_DIM), F32),
                pltpu.SemaphoreType.DMA((2,)),
            ],
        ),
        out_shape=[
            jax.ShapeDtypeStruct((t, da), BF16),
            jax.ShapeDtypeStruct((n_seq, N_HEADS, n_pages // PAGES_PER_BLOCK, HEAD_DIM), F32),
        ],
        compiler_params=_cparams(("arbitrary", "arbitrary", "arbitrary")),
        name="attn_prompt",
    )(pt_flat, q, kb, vb, kmean, cache_k)


def _gate_kernel(km_ref, q_ref, sel_ref):
    lq = q_ref.shape[0]
    nblk = km_ref.shape[2]
    ncol = N_HEADS * lq
    rows = lax.broadcasted_iota(I32, (ncol, HEAD_DIM), 0)
    g_t = jnp.zeros((nblk, ncol), F32)
    for hh in range(N_HEADS):
        qh = q_ref[:, hh * HEAD_DIM:(hh + 1) * HEAD_DIM]
        qpad = jnp.concatenate([qh] * N_HEADS, axis=0)
        qpad = jnp.where((rows >= hh * lq) & (rows < (hh + 1) * lq), qpad, 0.0)
        g_t = g_t + lax.dot_general(km_ref[0, hh], qpad, NT_DIMS, precision=HIGHEST,
                                    preferred_element_type=F32)
    valid = jnp.full(g_t.shape, True)
    _, rank = _topk_mask(g_t, valid)
    bidx = lax.broadcasted_iota(I32, g_t.shape, 0).astype(F32)
    out = []
    for r in range(MOBA_TOPK):
        out.append(jnp.sum(jnp.where(rank == r, bidx, 0.0), axis=0, keepdims=True))
    out.append(jnp.zeros((8 - MOBA_TOPK, ncol), F32))
    sel_ref[0] = jnp.concatenate(out, axis=0).astype(I32)


def _gate(kmean, q, n_seq, lq):
    nblk = kmean.shape[2]
    ncol = N_HEADS * lq
    return pl.pallas_call(
        _gate_kernel,
        grid=(n_seq,),
        in_specs=[
            pl.BlockSpec((1, N_HEADS, nblk, HEAD_DIM), lambda b: (b, 0, 0, 0)),
            pl.BlockSpec((lq, N_HEADS * HEAD_DIM), lambda b: (b, 0)),
        ],
        out_specs=pl.BlockSpec((1, 8, ncol), lambda b: (b, 0, 0)),
        out_shape=jax.ShapeDtypeStruct((n_seq, 8, ncol), I32),
        compiler_params=_cparams(("arbitrary",)),
        name="gate",
    )(kmean, q)


def _sattn_kernel(pt_ref, sel_ref, q_ref, kn_ref, vn_ref, ck_hbm, cv_hbm, o_ref, kbuf, vbuf, sem,
                  *, layer, n_pages, lq):
    s = pl.program_id(0)
    nsteps = pl.num_programs(0)
    nseg = lq * MOBA_TOPK

    def copies(step, slot):
        b = lax.shift_right_logical(step, 3)
        hh = step & (N_HEADS - 1)
        out = []
        for qq in range(lq):
            lanes = pl.ds(qq * HEAD_DIM, HEAD_DIM)
            for r in range(MOBA_TOPK):
                blk = sel_ref[step * nseg + qq * MOBA_TOPK + r]
                for half in range(PAGES_PER_BLOCK):
                    page = pt_ref[b * n_pages + blk * PAGES_PER_BLOCK + half]
                    rows = pl.ds(r * MOBA_BLOCK + half * PAGE_SIZE, PAGE_SIZE)
                    out.append(pltpu.make_async_copy(ck_hbm.at[layer, page, :, hh, :], kbuf.at[slot, rows, lanes], sem.at[0, slot]))
                    out.append(pltpu.make_async_copy(cv_hbm.at[layer, page, :, hh, :], vbuf.at[slot, rows, lanes], sem.at[1, slot]))
        return out

    @pl.when(s == 0)
    def _():
        for cp in copies(s, 0):
            cp.start()

    @pl.when(s + 1 < nsteps)
    def _():
        for cp in copies(s + 1, (s + 1) & 1):
            cp.start()

    slot = s & 1
    for cp in copies(s, slot):
        cp.wait()

    qb = (q_ref[...] * SM_SCALE).astype(BF16)
    q_bd = jnp.concatenate([qb] * lq, axis=1)
    rq = lax.broadcasted_iota(I32, q_bd.shape, 0)
    cq = lax.broadcasted_iota(I32, q_bd.shape, 1)
    own_lanes = (cq >= rq * HEAD_DIM) & (cq < (rq + 1) * HEAD_DIM)
    q_bd = jnp.where(own_lanes, q_bd, jnp.zeros_like(q_bd))
    s_t = lax.dot_general(kbuf[slot].astype(BF16), q_bd, NT_DIMS, preferred_element_type=F32)
    sn_t = lax.dot_general(kn_ref[...].astype(BF16), qb, NT_DIMS, preferred_element_type=F32)
    rn = lax.broadcasted_iota(I32, sn_t.shape, 0)
    cn = lax.broadcasted_iota(I32, sn_t.shape, 1)
    sn_t = jnp.where(rn <= cn, sn_t, NEG)
    m = jnp.maximum(jnp.max(s_t, axis=0, keepdims=True), jnp.max(sn_t, axis=0, keepdims=True))
    p = jnp.exp(s_t - m)
    pn = jnp.exp(sn_t - m)
    inv_l = 1.0 / (jnp.sum(p, axis=0, keepdims=True) + jnp.sum(pn, axis=0, keepdims=True))
    p = p * inv_l
    pn = pn * inv_l
    vn = vn_ref[...]
    rows = []
    for qq in range(lq):
        lanes = slice(qq * HEAD_DIM, (qq + 1) * HEAD_DIM)
        o = jnp.sum(p[:, qq:qq + 1] * vbuf[slot, :, lanes], axis=0, keepdims=True)
        o = o + jnp.sum(pn[:, qq:qq + 1] * vn, axis=0, keepdims=True)
        rows.append(o)
    o_ref[...] = jnp.concatenate(rows, axis=0)


def _sattn(pt_flat, sel_flat, q, kn, vn, cache_k, cache_v, layer, n_seq, lq, n_pages):
    nseg = lq * MOBA_TOPK
    blk = lambda s, pt, sel: (lax.shift_right_logical(s, 3), s & (N_HEADS - 1))
    spec = pl.BlockSpec((lq, HEAD_DIM), blk)
    return pl.pallas_call(
        functools.partial(_sattn_kernel, layer=layer, n_pages=n_pages, lq=lq),
        grid_spec=pltpu.PrefetchScalarGridSpec(
            num_scalar_prefetch=2,
            grid=(n_seq * N_HEADS,),
            in_specs=[spec, spec, spec, pl.BlockSpec(memory_space=pl.ANY), pl.BlockSpec(memory_space=pl.ANY)],
            out_specs=spec,
            scratch_shapes=[
                pltpu.VMEM((2, MOBA_TOPK * MOBA_BLOCK, lq * HEAD_DIM), F32),
                pltpu.VMEM((2, MOBA_TOPK * MOBA_BLOCK, lq * HEAD_DIM), F32),
                pltpu.SemaphoreType.DMA((2, 2)),
            ],
        ),
        out_shape=jax.ShapeDtypeStruct(q.shape, F32),
        compiler_params=_cparams(("arbitrary",)),
        name="sattn",
    )(pt_flat, sel_flat, q, kn, vn, cache_k, cache_v)


COL_CHUNK = 512


def _merge_kernel(h_ref, wg_ref, py_ref, wpp_ref, at_ref, wap_ref, o_ref):
    d = o_ref.shape[1]
    h = h_ref[...]
    py = py_ref[...]
    at = at_ref[...]
    for c0 in range(0, d, COL_CHUNK):
        cols = slice(c0, c0 + COL_CHUNK)
        g1 = jax.nn.sigmoid(jnp.dot(h, wg_ref[:, cols], preferred_element_type=F32))
        g2 = jax.nn.sigmoid(jnp.dot(h, wg_ref[:, d + c0:d + c0 + COL_CHUNK], preferred_element_type=F32))
        a = jnp.dot(py, wpp_ref[:, cols], preferred_element_type=F32)
        b = jnp.dot(at, wap_ref[:, cols], preferred_element_type=F32)
        o_ref[:, cols] = (g1 * a + g2 * b).astype(BF16)


def _merge(h, wgate, py, wpp, at, wap, layer, tm):
    t, d = h.shape
    dp = py.shape[1]
    da = at.shape[1]
    row = lambda i: (i, 0)
    lay = lambda i: (layer, 0, 0)
    return pl.pallas_call(
        _merge_kernel,
        grid=(t // tm,),
        in_specs=[
            pl.BlockSpec((tm, d), row),
            _resident((None, d, 2 * d), lay),
            pl.BlockSpec((tm, dp), row),
            _resident((None, dp, d), lay),
            pl.BlockSpec((tm, da), row),
            _resident((None, da, d), lay),
        ],
        out_specs=pl.BlockSpec((tm, d), row),
        out_shape=jax.ShapeDtypeStruct((t, d), BF16),
        compiler_params=_cparams(("arbitrary",)),
        name="merge",
    )(h, wgate, py, wpp, at, wap)


def _outproj_kernel(a_ref, w_ref, x_ref, g_ref, x1_ref, h2_ref):
    d = x1_ref.shape[1]
    a = a_ref[...]
    for c0 in range(0, d, COL_CHUNK):
        cols = slice(c0, c0 + COL_CHUNK)
        x1_ref[:, cols] = x_ref[:, cols] + jnp.dot(a, w_ref[:, cols], preferred_element_type=F32)
    h2_ref[...] = _rms(x1_ref[...], g_ref[...]).astype(BF16)


def _outproj(a, w, x, g, layer, tm):
    t, kd = a.shape
    d = w.shape[2]
    row = lambda i: (i, 0)
    return pl.pallas_call(
        _outproj_kernel,
        grid=(t // tm,),
        in_specs=[
            pl.BlockSpec((tm, kd), row),
            _resident((None, kd, d), lambda i: (layer, 0, 0)),
            pl.BlockSpec((tm, d), row),
            _resident((None, 1, d), lambda i: (layer, 0, 0)),
        ],
        out_specs=[pl.BlockSpec((tm, d), row), pl.BlockSpec((tm, d), row)],
        out_shape=[jax.ShapeDtypeStruct((t, d), F32), jax.ShapeDtypeStruct((t, d), BF16)],
        compiler_params=_cparams(("arbitrary",)),
        name="outproj",
    )(a, w, x, g)


def _mlp_up_kernel(h2_ref, w_ref, o_ref):
    n = o_ref.shape[1]
    h2 = h2_ref[...]
    for c0 in range(0, n, COL_CHUNK):
        cols = slice(c0, c0 + COL_CHUNK)
        acc = jnp.dot(h2, w_ref[:, cols], preferred_element_type=F32)
        o_ref[:, cols] = jnp.square(jnp.maximum(acc, 0.0)).astype(BF16)


def _mlp_up(h2, w, layer, tm):
    t, d = h2.shape
    n = w.shape[2]
    row = lambda i: (i, 0)
    return pl.pallas_call(
        _mlp_up_kernel,
        grid=(t // tm,),
        in_specs=[pl.BlockSpec((tm, d), row), _resident((None, d, n), lambda i: (layer, 0, 0))],
        out_specs=pl.BlockSpec((tm, n), row),
        out_shape=jax.ShapeDtypeStruct((t, n), BF16),
        compiler_params=_cparams(("arbitrary",)),
        name="mlp_up",
    )(h2, w)


def _mlp_down_kernel(a_ref, w_ref, x_ref, g_ref, o_ref, *, normed):
    d = o_ref.shape[1]
    a = a_ref[...]
    for c0 in range(0, d, COL_CHUNK):
        cols = slice(c0, c0 + COL_CHUNK)
        o_ref[:, cols] = x_ref[:, cols] + jnp.dot(a, w_ref[:, cols], preferred_element_type=F32)
    if normed:
        o_ref[...] = _rms(o_ref[...], g_ref[...])


def _mlp_down(a, w, x, g, layer, tm, normed):
    t, n = a.shape
    d = w.shape[2]
    row = lambda i: (i, 0)
    return pl.pallas_call(
        functools.partial(_mlp_down_kernel, normed=normed),
        grid=(t // tm,),
        in_specs=[
            pl.BlockSpec((tm, n), row),
            _resident((None, n, d), lambda i: (layer, 0, 0)),
            pl.BlockSpec((tm, d), row),
            _resident((1, d), lambda i: (0, 0)),
        ],
        out_specs=pl.BlockSpec((tm, d), row),
        out_shape=jax.ShapeDtypeStruct((t, d), F32),
        compiler_params=_cparams(("arbitrary",)),
        name="mlp_down",
    )(a, w, x, g)


def _rope_tables(pos):
    inv = ROPE_THETA ** (-np.arange(0, HEAD_DIM, 2, dtype=np.float64) / HEAD_DIM)
    ang = np.asarray(pos, np.float64)[:, None] * inv[None, :]
    cos = np.concatenate([np.cos(ang), np.cos(ang)], axis=1)
    sin = np.concatenate([-np.sin(ang), np.sin(ang)], axis=1)
    return jnp.asarray(cos, F32), jnp.asarray(sin, F32)


def _row_layers(x, h, py, at, weights, g_final, layer, tm):
    wgate, wpp, wap, wout, gmlp, wup, wdown = weights
    merged = _merge(h, wgate, py, wpp, at, wap, layer, tm)
    x1, h2 = _outproj(merged, wout, x, gmlp, layer, tm)
    a = _mlp_up(h2, wup, layer, tm)
    return _mlp_down(a, wdown, x1, g_final, layer, tm, normed=(layer == wdown.shape[0] - 1))


def kernel(x_prompt, x_sample, cache_k, cache_v, state_pool, page_table, g_mix, w_in, w_pool_grp, pool_scale, w_pool_proj, w_att_proj, w_gate, w_out, g_mlp, w_up, w_down, g_final):
    bp, sp, d = x_prompt.shape
    bs, ls, _ = x_sample.shape
    depth = w_in.shape[0]
    n_pages = page_table.shape[1]
    past = n_pages * PAGE_SIZE
    dp = pool_scale.shape[1]
    tp = ROW_TILE
    assert tp == MOBA_BLOCK
    ts_rows = bs * ls

    xp = x_prompt.reshape(bp * sp, d)
    xs = x_sample.reshape(ts_rows, d)
    cos_p, sin_p = _rope_tables(np.arange(sp))
    cos_s, sin_s = _rope_tables(np.tile(past + np.arange(ls), bs))
    pt_flat = page_table.reshape(-1)
    prev_p = jnp.zeros((bp, POOL_HALO, dp), F32)

    gm = g_mix.reshape(depth, 1, d)
    win = w_in.astype(BF16)
    wgrp = w_pool_grp.astype(BF16)
    psc = pool_scale.reshape(depth, 1, dp)
    rest = (w_gate.astype(BF16), w_pool_proj.astype(BF16), w_att_proj.astype(BF16),
            w_out.astype(BF16), g_mlp.reshape(depth, 1, d), w_up.astype(BF16), w_down.astype(BF16))

    gf = g_final.reshape(1, d)

    kp_l, vp_l, pp_l, ks_l, vs_l, ps_l = [], [], [], [], [], []
    for l in range(depth):
        hp, up, qp, kp, vp, kbp, vbp, kmp = _inproj(xp, gm, win, cos_p, sin_p, l, tp)
        up3 = up.reshape(bp, sp, dp)
        pyp = _pool(up3, prev_p, wgrp, psc, l, POOL_TILE, 0).reshape(bp * sp, dp)
        atp, kmean_cache = _attn_prompt(pt_flat, qp, kbp, vbp, kmp.reshape(bp * sp // MOBA_BLOCK, -1),
                                        cache_k, l, bp, sp, bs, n_pages)
        xp = _row_layers(xp, hp, pyp, atp, rest, gf, l, tp)
        kp_l.append(kp.reshape(bp, sp // PAGE_SIZE, PAGE_SIZE, N_HEADS, HEAD_DIM))
        vp_l.append(vp.reshape(bp, sp // PAGE_SIZE, PAGE_SIZE, N_HEADS, HEAD_DIM))
        pp_l.append(up3[:, sp - POOL_STATE:, :])

        hs, us, qs, ks, vs, _, _, _ = _inproj(xs, gm, win, cos_s, sin_s, l, ts_rows)
        us3 = us.reshape(bs, ls, dp)
        prev_s = jnp.pad(state_pool[l], ((0, 0), (POOL_HALO - POOL_STATE, 0), (0, 0)))
        pys = _pool(us3, prev_s, wgrp, psc, l, ls, past).reshape(ts_rows, dp)
        sel = _gate(kmean_cache, qs, bs, ls)
        sel_flat = sel[:, :MOBA_TOPK, :].transpose(0, 2, 1).reshape(-1)
        ats = _sattn(pt_flat, sel_flat, qs, ks, vs, cache_k, cache_v, l, bs, ls, n_pages).astype(BF16)
        xs = _row_layers(xs, hs, pys, ats, rest, gf, l, ts_rows)
        ks_l.append(ks.reshape(bs, ls, N_HEADS, HEAD_DIM))
        vs_l.append(vs.reshape(bs, ls, N_HEADS, HEAD_DIM))
        ps_l.append(jnp.concatenate([state_pool[l], us3], axis=1)[:, -POOL_STATE:, :])

    y_prompt = xp.reshape(bp, sp, d)
    y_sample = xs.reshape(bs, ls, d)
    return (y_prompt, y_sample, jnp.stack(kp_l), jnp.stack(vp_l), jnp.stack(pp_l),
            jnp.stack(ks_l), jnp.stack(vs_l), jnp.stack(ps_l))
```
